```python
import math
import jax, jax.numpy as jnp
from jax import lax
import numpy as np

D_MODEL = 1024
BATCH = 2
SEQ = 8192
DEPTH = 1

HEAD_DIM = 64
N_HEADS_A = 8
N_HEADS_B = 8
MIX_WIDTH = (N_HEADS_A + N_HEADS_B) * HEAD_DIM
DILATED_PAIRS = ((128, 1), (512, 4), (2048, 16))
IDX_HEADS = 8
IDX_DIM = 64
TOPK_MAX = 256
Q_BLOCK = 128
D_FF = 2816
CONV_WIDTH = 3
PLE_DIM = 256
ROPE_THETA = 10000.0
RMS_EPS = 1e-6

A_W = N_HEADS_A * HEAD_DIM
B_W = N_HEADS_B * HEAD_DIM
IQ_W = IDX_HEADS * IDX_DIM
IN_COLS = 3 * A_W + 3 * B_W + IQ_W + IDX_DIM + IDX_HEADS

kernel_name = "hybrid_dilated_dsa_convffn_ple"


def rmsnorm(x, g):
    x32 = x.astype(jnp.float32)
    y = x32 * lax.rsqrt(jnp.mean(x32 * x32, axis=-1, keepdims=True) + RMS_EPS)
    return (y * g.astype(jnp.float32)).astype(x.dtype)


def rope_tables(positions):
    inv_freq = 1.0 / (ROPE_THETA ** (jnp.arange(0, HEAD_DIM, 2, dtype=jnp.float32) / HEAD_DIM))
    ang = positions.astype(jnp.float32)[..., None] * inv_freq
    return jnp.cos(ang)[:, :, None, :], jnp.sin(ang)[:, :, None, :]


def apply_rope(x, cos, sin):
    x32 = x.astype(jnp.float32)
    x1, x2 = jnp.split(x32, 2, axis=-1)
    out = jnp.concatenate([x1 * cos - x2 * sin, x2 * cos + x1 * sin], axis=-1)
    return out.astype(x.dtype)


def dilated_branch(q, k, v, window, dil):
    B, T, H, Dh = q.shape
    n = window // dil
    seg = dil * n
    Tp = -(-T // seg) * seg
    nb = Tp // seg

    def split(a):
        a = jnp.pad(a, ((0, 0), (0, Tp - T), (0, 0), (0, 0)))
        a = a.reshape(B, Tp // dil, dil, H, Dh).transpose(0, 2, 1, 3, 4)
        return a.reshape(B, dil, nb, n, H, Dh)

    def with_prev(a):
        prev = jnp.pad(a, ((0, 0), (0, 0), (1, 0), (0, 0), (0, 0), (0, 0)))[:, :, :-1]
        return jnp.concatenate([prev, a], axis=3)

    qs = split(q)
    ks = with_prev(split(k))
    vs = with_prev(split(v))
    s = jnp.einsum('brnqhd,brnkhd->brnhqk', qs, ks).astype(jnp.float32) * (HEAD_DIM ** -0.5)
    a_idx = jnp.arange(n)[:, None]
    c_idx = jnp.arange(2 * n)[None, :]
    band = (c_idx >= a_idx) & (c_idx <= a_idx + n)
    not_first = (jnp.arange(nb) > 0)[:, None, None]
    mask = band[None] & (not_first | (c_idx >= n)[None])
    s = jnp.where(mask[None, None, :, None], s, -jnp.inf)
    lse = jax.nn.logsumexp(s, axis=-1)
    pr = jnp.exp(s - lse[..., None])
    o = jnp.einsum('brnhqk,brnkhd->brnqhd', pr.astype(v.dtype), vs)
    o = o.reshape(B, dil, Tp // dil, H, Dh).transpose(0, 2, 1, 3, 4).reshape(B, Tp, H, Dh)[:, :T]
    lse = lse.transpose(0, 1, 2, 4, 3).reshape(B, dil, Tp // dil, H).transpose(0, 2, 1, 3)
    lse = lse.reshape(B, Tp, H)[:, :T]
    return o, lse


def dilated_mixture(q, k, v):
    outs, lses = [], []
    for window, dil in DILATED_PAIRS:
        o, l = dilated_branch(q, k, v, window, dil)
        outs.append(o)
        lses.append(l)
    wts = jax.nn.softmax(jnp.stack(lses, axis=0), axis=0)
    o = jnp.sum(wts[..., None] * jnp.stack(outs, axis=0).astype(jnp.float32), axis=0)
    return o.astype(q.dtype)


def sparse_attention(q, k, v, qi, ki, wi, topk):
    B, T, H, Dh = q.shape
    nb = T // Q_BLOCK

    def blk(a):
        return a.reshape((B, nb, Q_BLOCK) + a.shape[2:]).swapaxes(0, 1)

    key_pos = jnp.arange(T)

    def one_block(args):
        qb, qib, wib, start = args
        t = start + jnp.arange(Q_BLOCK)
        sc = jnp.einsum('bqhd,bkd->bqhk', qib, ki).astype(jnp.float32) * (IDX_DIM ** -0.5)
        score = jnp.einsum('bqhk,bqh->bqk', jax.nn.relu(sc),
                           wib.astype(jnp.float32) * (IDX_HEADS ** -0.5))
        causal = key_pos[None, :] <= t[:, None]
        score = jnp.where(causal[None], score, -jnp.inf)
        _, sel = lax.top_k(score, topk)
        valid = sel <= t[None, :, None]
        kg = jax.vmap(lambda a, i: a[i])(k, sel)
        vg = jax.vmap(lambda a, i: a[i])(v, sel)
        s = jnp.einsum('bqhd,bqkhd->bhqk', qb, kg).astype(jnp.float32) * (HEAD_DIM ** -0.5)
        s = jnp.where(valid[:, None], s, -jnp.inf)
        pr = jax.nn.softmax(s, axis=-1)
        return jnp.einsum('bhqk,bqkhd->bqhd', pr.astype(v.dtype), vg)

    out = lax.map(one_block, (blk(q), blk(qi), blk(wi), jnp.arange(nb) * Q_BLOCK))
    return out.swapaxes(0, 1).reshape(B, T, H, Dh)


def causal_depthwise_conv(u, w, b):
    C = u.shape[-1]
    up = jnp.pad(u, ((0, 0), (CONV_WIDTH - 1, 0), (0, 0)))
    y = lax.conv_general_dilated(up, w[:, None, :].astype(u.dtype), window_strides=(1,),
                                 padding='VALID', dimension_numbers=('NWC', 'WIO', 'NWC'),
                                 feature_group_count=C)
    return y + b.astype(u.dtype)


def setup_inputs(seed: int = 0) -> dict:
    key = jax.random.key(seed)
    ks = jax.random.split(key, 16)
    f32 = jnp.float32
    nrm = lambda k, shape, scale: jax.random.normal(k, shape, f32) * scale
    x = nrm(ks[0], (BATCH, SEQ, D_MODEL), 1.0)
    p = nrm(ks[1], (DEPTH, BATCH, SEQ, PLE_DIM), 1.0)
    offsets = jax.random.randint(ks[2], (BATCH, 1), 0, 4096, dtype=jnp.int32)
    positions = offsets + jnp.arange(SEQ, dtype=jnp.int32)[None, :]
    attn_norm = 1.0 + nrm(ks[3], (DEPTH, D_MODEL), 0.02)
    w_in = nrm(ks[4], (DEPTH, D_MODEL, IN_COLS), D_MODEL ** -0.5)
    w_o = nrm(ks[5], (DEPTH, MIX_WIDTH, D_MODEL), MIX_WIDTH ** -0.5)
    ffn_norm = 1.0 + nrm(ks[6], (DEPTH, D_MODEL), 0.02)
    w_up = nrm(ks[7], (DEPTH, D_MODEL, 2 * D_FF), D_MODEL ** -0.5)
    conv_w = nrm(ks[8], (DEPTH, CONV_WIDTH, 2 * D_FF), CONV_WIDTH ** -0.5)
    conv_b = nrm(ks[9], (DEPTH, 2 * D_FF), 0.01)
    w_down = nrm(ks[10], (DEPTH, D_FF, D_MODEL), D_FF ** -0.5)
    ple_norm = 1.0 + nrm(ks[11], (DEPTH, D_MODEL), 0.02)
    w_ple_gate = nrm(ks[12], (DEPTH, D_MODEL, D_MODEL), D_MODEL ** -0.5)
    w_ple_proj = nrm(ks[13], (DEPTH, PLE_DIM, D_MODEL), PLE_DIM ** -0.5)
    final_norm = 1.0 + nrm(ks[14], (D_MODEL,), 0.02)
    return {"x": x, "p": p, "positions": positions, "attn_norm": attn_norm, "w_in": w_in,
            "w_o": w_o, "ffn_norm": ffn_norm, "w_up": w_up, "conv_w": conv_w, "conv_b": conv_b,
            "w_down": w_down, "ple_norm": ple_norm, "w_ple_gate": w_ple_gate,
            "w_ple_proj": w_ple_proj, "final_norm": final_norm}


def reference(x, p, positions, attn_norm, w_in, w_o, ffn_norm, w_up, conv_w, conv_b,
              w_down, ple_norm, w_ple_gate, w_ple_proj, final_norm):
    B, T, _ = x.shape
    topk = min(TOPK_MAX, T // 4)
    cos, sin = rope_tables(positions)
    splits = np.cumsum([A_W, A_W, A_W, B_W, B_W, B_W, IQ_W, IDX_DIM])
    for i in range(DEPTH):
        h = rmsnorm(x, attn_norm[i])
        z = h @ w_in[i]
        qa, ka, va, qb, kb, vb, qi, ki, wi = jnp.split(z, splits, axis=-1)
        heads = lambda a, n, d: a.reshape(B, T, n, d)
        qa = apply_rope(heads(qa, N_HEADS_A, HEAD_DIM), cos, sin)
        ka = apply_rope(heads(ka, N_HEADS_A, HEAD_DIM), cos, sin)
        va = heads(va, N_HEADS_A, HEAD_DIM)
        qb = apply_rope(heads(qb, N_HEADS_B, HEAD_DIM), cos, sin)
        kb = apply_rope(heads(kb, N_HEADS_B, HEAD_DIM), cos, sin)
        vb = heads(vb, N_HEADS_B, HEAD_DIM)
        qi = apply_rope(heads(qi, IDX_HEADS, IDX_DIM), cos, sin)
        ki = apply_rope(heads(ki, 1, IDX_DIM), cos, sin)[:, :, 0]
        out_a = dilated_mixture(qa, ka, va)
        out_b = sparse_attention(qb, kb, vb, qi, ki, wi, topk)
        mix = jnp.concatenate([out_a.reshape(B, T, A_W), out_b.reshape(B, T, B_W)], axis=-1)
        x = x + mix @ w_o[i]
        u = rmsnorm(x, ffn_norm[i]) @ w_up[i]
        u = causal_depthwise_conv(u, conv_w[i], conv_b[i])
        g, up = jnp.split(u, 2, axis=-1)
        x = x + (jax.nn.silu(g) * up) @ w_down[i]
        gate = jax.nn.sigmoid(rmsnorm(x, ple_norm[i]) @ w_ple_gate[i])
        x = x + gate * (p[i] @ w_ple_proj[i])
    return rmsnorm(x, final_norm)
```

```python
import functools
import math

import numpy as np
import jax
import jax.numpy as jnp
from jax import lax
from jax.experimental import pallas as pl
from jax.experimental.pallas import tpu as pltpu

F32 = jnp.float32
BF16 = jnp.bfloat16

HEAD_DIM = 64
HALF = HEAD_DIM // 2
N_HEADS = 8
GROUP_W = N_HEADS * HEAD_DIM
N_PAIRS = N_HEADS // 2
LANES = 128
DILATED_PAIRS = ((128, 1), (512, 4), (2048, 16))
WIN_STEPS = 128
TOPK_MAX = 256
Q_BLOCK = 128
D_FF = 2816
FF_CHUNK = 256
N_FF_CHUNKS = D_FF // FF_CHUNK
CONV_WIDTH = 3
ROPE_THETA = 10000.0
RMS_EPS = 1e-6
MASK_VALUE = -1e30
KEY_NEG_INF = -2139095041
VMEM_LIMIT = 56 * 1024 * 1024


def _cparams(n_axes):
    return pltpu.CompilerParams(dimension_semantics=("arbitrary",) * n_axes,
                                vmem_limit_bytes=VMEM_LIMIT)


def _dot_t(a, b):
    return lax.dot_general(a, b, (((1,), (1,)), ((), ())), preferred_element_type=F32)


def _dot(a, b):
    return jnp.dot(a, b, preferred_element_type=F32)


def _rms(x, g):
    ms = jnp.mean(x * x, axis=-1, keepdims=True)
    return (x * lax.rsqrt(ms + RMS_EPS)) * g


def _pair_perm():
    idx = []
    for j in range(N_PAIRS):
        a, b = 2 * j * HEAD_DIM, (2 * j + 1) * HEAD_DIM
        idx += list(range(a, a + HALF)) + list(range(b, b + HALF))
        idx += list(range(a + HALF, a + HEAD_DIM)) + list(range(b + HALF, b + HEAD_DIM))
    return np.asarray(idx, np.int32)


def _in_proj_columns():
    perm = _pair_perm()
    nat = np.arange(GROUP_W, dtype=np.int32)
    cols = []
    for g, roped in enumerate((True, True, False, True, True, False, True)):
        cols.append(g * GROUP_W + (perm if roped else nat))
    kbase = 7 * GROUP_W
    k1 = kbase + np.arange(HALF, dtype=np.int32)
    k2 = kbase + HALF + np.arange(HALF, dtype=np.int32)
    cols.append(np.concatenate([k1, k1, k2, k2]))
    return np.concatenate(cols)


def _proj_kernel(x_ref, pos_ref, g_ref, invf_ref, w_ref, wiw_ref, za_ref, zb_ref, zi_ref, wi_ref):
    x = x_ref[...]
    h32 = _rms(x, g_ref[...])
    h = h32.astype(BF16)
    ang = pos_ref[...].astype(F32) * invf_ref[...]
    lane = lax.broadcasted_iota(jnp.int32, (1, LANES), 1)
    cos = jnp.cos(ang)
    sin = jnp.sin(ang) * jnp.where(lane < 2 * HALF, -1.0, 1.0)

    def rope(z):
        return z * cos + pltpu.roll(z, 2 * HALF, 1) * sin

    q_scale = HEAD_DIM ** -0.5
    plan = ((za_ref, 0, True, q_scale), (za_ref, 1, True, 1.0), (za_ref, 2, False, 1.0),
            (zb_ref, 0, True, q_scale), (zb_ref, 1, True, 1.0), (zb_ref, 2, False, 1.0),
            (zi_ref, 0, True, q_scale))
    for g, (out_ref, slot, roped, scale) in enumerate(plan):
        z = _dot(h, w_ref[:, g * GROUP_W:(g + 1) * GROUP_W])
        for j in range(N_PAIRS):
            zj = z[:, j * LANES:(j + 1) * LANES]
            if roped:
                zj = rope(zj)
            if scale != 1.0:
                zj = zj * scale
            out_ref[:, slot * GROUP_W + j * LANES: slot * GROUP_W + (j + 1) * LANES] = zj.astype(BF16)
    zk = _dot(h, w_ref[:, 7 * GROUP_W: 7 * GROUP_W + LANES])
    zi_ref[:, GROUP_W:GROUP_W + LANES] = rope(zk).astype(BF16)
    wi_ref[...] = _dot(h, wiw_ref[...]) * (N_HEADS ** -0.5)


def _project(x2, pos2, attn_norm, w_in, tm):
    n, d = x2.shape
    cols = _in_proj_columns()
    w_main = jnp.take(w_in, cols, axis=1).astype(BF16)
    w_wi = jnp.pad(w_in[:, 7 * GROUP_W + HEAD_DIM:], ((0, 0), (0, LANES - N_HEADS))).astype(BF16)
    inv_freq = 1.0 / (ROPE_THETA ** (np.arange(0, HEAD_DIM, 2, dtype=np.float32) / HEAD_DIM))
    invf = jnp.asarray(np.tile(inv_freq, 4)[None, :], F32)
    wcols = w_main.shape[1]
    const = lambda i: (0, 0)
    row = lambda i: (i, 0)
    return pl.pallas_call(
        _proj_kernel,
        grid=(n // tm,),
        in_specs=[pl.BlockSpec((tm, d), row), pl.BlockSpec((tm, 1), row),
                  pl.BlockSpec((1, d), const), pl.BlockSpec((1, LANES), const),
                  pl.BlockSpec((d, wcols), const), pl.BlockSpec((d, LANES), const)],
        out_specs=[pl.BlockSpec((tm, 3 * GROUP_W), row), pl.BlockSpec((tm, 3 * GROUP_W), row),
                   pl.BlockSpec((tm, GROUP_W + LANES), row), pl.BlockSpec((tm, LANES), row)],
        out_shape=[jax.ShapeDtypeStruct((n, 3 * GROUP_W), BF16),
                   jax.ShapeDtypeStruct((n, 3 * GROUP_W), BF16),
                   jax.ShapeDtypeStruct((n, GROUP_W + LANES), BF16),
                   jax.ShapeDtypeStruct((n, LANES), F32)],
        compiler_params=_cparams(1), name="in_proj_rope",
    )(x2, pos2, attn_norm.reshape(1, d), invf, w_main, w_wi)


def _head_lane_masks():
    lane = lax.broadcasted_iota(jnp.int32, (1, LANES), 1)
    first = (lane % HEAD_DIM) < HALF
    return first, jnp.logical_not(first)


def _dilated_kernel(q_ref, kc_ref, kp_ref, vc_ref, vp_ref, o_ref, lse_ref):
    nb = pl.program_id(2)
    n = WIN_STEPS
    a = lax.broadcasted_iota(jnp.int32, (n, 2 * n), 0)
    c = lax.broadcasted_iota(jnp.int32, (n, 2 * n), 1)
    band = (c >= a) & (c <= a + n) & ((nb > 0) | (c >= n))
    masks = _head_lane_masks()
    lane = lax.broadcasted_iota(jnp.int32, (1, LANES), 1)
    low = lane < HEAD_DIM
    for j in range(N_PAIRS):
        sl = slice(j * LANES, (j + 1) * LANES)
        q = q_ref[:, sl]
        kk = jnp.concatenate([kp_ref[:, sl], kc_ref[:, sl]], axis=0)
        vv = jnp.concatenate([vp_ref[:, sl], vc_ref[:, sl]], axis=0)
        outs, lses = [], []
        for hb in range(2):
            qm = jnp.where(masks[hb], q, jnp.zeros_like(q))
            s = _dot_t(qm, kk)
            s = jnp.where(band, s, -jnp.inf)
            m = jnp.max(s, axis=1, keepdims=True)
            p = jnp.exp(s - m)
            l = jnp.sum(p, axis=1, keepdims=True)
            outs.append(_dot(p.astype(BF16), vv) / l)
            lses.append(m + jnp.log(l))
        o_ref[:, sl] = jnp.where(low, outs[0], outs[1])
        lse_ref[:, sl] = jnp.where(low, lses[0], lses[1])


def _dilated_branch(za, b, t, dil):
    n = WIN_STEPS
    rows = t // dil
    zv = za.reshape(b, rows, dil * 3 * GROUP_W)
    blk = (None, n, GROUP_W)
    cur = lambda slot: (lambda bi, r, nb: (bi, nb, r * 3 + slot))
    prev = lambda slot: (lambda bi, r, nb: (bi, jnp.maximum(nb - 1, 0), r * 3 + slot))
    out_map = lambda bi, r, nb: (bi, nb, r)
    o, lse = pl.pallas_call(
        _dilated_kernel,
        grid=(b, dil, rows // n),
        in_specs=[pl.BlockSpec(blk, cur(0)), pl.BlockSpec(blk, cur(1)), pl.BlockSpec(blk, prev(1)),
                  pl.BlockSpec(blk, cur(2)), pl.BlockSpec(blk, prev(2))],
        out_specs=[pl.BlockSpec(blk, out_map), pl.BlockSpec(blk, out_map)],
        out_shape=[jax.ShapeDtypeStruct((b, rows, dil * GROUP_W), F32)] * 2,
        compiler_params=_cparams(3), name=f"dilated_attn_d{dil}",
    )(zv, zv, zv, zv, zv)
    return o.reshape(b * t, GROUP_W), lse.reshape(b * t, GROUP_W)


P1_CHUNK = 256
P3_CHUNK = 256


def _dsa_kernel(qb_ref, qi_ref, wi_ref, kb_ref, vb_ref, ki_ref, o_ref,
                key_ref, qim_ref, qbm_ref, wbc_ref, m_ref, l_ref, acc_ref, thr_ref, *, topk):
    tq = Q_BLOCK
    qblk = pl.program_id(1)
    t0 = qblk * tq
    n_chunks = (qblk + 2) // 2
    masks = _head_lane_masks()

    for h in range(N_HEADS):
        j, hb = divmod(h, 2)
        sl = slice(j * LANES, (j + 1) * LANES)
        qi = qi_ref[:, sl]
        qb = qb_ref[:, sl]
        qim_ref[h] = jnp.where(masks[hb], qi, jnp.zeros_like(qi))
        qbm_ref[h] = jnp.where(masks[hb], qb, jnp.zeros_like(qb))
        wbc_ref[h] = jnp.broadcast_to(wi_ref[:, h:h + 1], (tq, LANES))

    row_t = t0 + lax.broadcasted_iota(jnp.int32, (tq, P1_CHUNK), 0)
    col_i = lax.broadcasted_iota(jnp.int32, (tq, P1_CHUNK), 1)

    def p1(c, carry):
        k0 = pl.multiple_of(c * P1_CHUNK, P1_CHUNK)
        kic = ki_ref[pl.ds(k0, P1_CHUNK), :]
        acc = jnp.zeros((tq, P1_CHUNK), F32)
        for h in range(N_HEADS):
            sc = _dot_t(qim_ref[h], kic)
            w = wbc_ref[h]
            acc = acc + jnp.concatenate([w, w], axis=1) * jnp.maximum(sc, 0.0)
        acc = jnp.where(col_i + k0 <= row_t, acc, -jnp.inf)
        bits = pltpu.bitcast(acc, jnp.int32)
        key_ref[:, pl.ds(k0, P1_CHUNK)] = bits ^ ((bits >> 31) & 0x7FFFFFFF)
        return carry

    lax.fori_loop(0, n_chunks, p1, 0)

    def count_ge(thr):
        thr_b = jnp.broadcast_to(thr, (tq, LANES))

        def body(c, cnt):
            k0 = pl.multiple_of(c * P1_CHUNK, P1_CHUNK)
            for u in range(P1_CHUNK // LANES):
                kc = key_ref[:, pl.ds(k0 + u * LANES, LANES)]
                cnt = cnt + jnp.where(kc >= thr_b, 1, 0)
            return cnt

        cnt = lax.fori_loop(0, n_chunks, body, jnp.zeros((tq, LANES), jnp.int32))
        return jnp.sum(cnt, axis=1, keepdims=True)

    t_col = t0 + lax.broadcasted_iota(jnp.int32, (tq, 1), 0)
    n_causal = t_col + 1
    all_sel = n_causal <= topk
    lo0 = jnp.full((tq, 1), KEY_NEG_INF + 1, jnp.int32)
    hi0 = jnp.full((tq, 1), 0x7FFFFFFF, jnp.int32)
    thr0 = lo0
    done0 = all_sel.astype(jnp.int32)
    cnt_hi0 = jnp.zeros((tq, 1), jnp.int32)

    def bis_cond(st):
        _, _, _, done, _ = st
        return jnp.min(done) == 0

    def bis_body(st):
        lo, hi, thr, done, cnt_hi = st
        mid = (lo >> 1) + (hi >> 1) + (((lo & 1) + (hi & 1) + 1) >> 1)
        c = count_ge(mid)
        ge = c >= topk
        active = done == 0
        hit = active & (c == topk)
        new_lo = jnp.where(active & ge, mid, lo)
        new_hi = jnp.where(active & jnp.logical_not(ge), mid - 1, hi)
        new_cnt_hi = jnp.where(active & jnp.logical_not(ge), c, cnt_hi)
        conv = active & jnp.logical_not(hit) & (new_lo == new_hi)
        new_thr = jnp.where(hit, mid, jnp.where(conv, new_lo, thr))
        new_done = jnp.where(hit, 1, jnp.where(conv, 2, done))
        return new_lo, new_hi, new_thr, new_done, new_cnt_hi

    _, _, thr, done, cnt_hi = lax.while_loop(bis_cond, bis_body, (lo0, hi0, thr0, done0, cnt_hi0))
    thr_ref[...] = jnp.broadcast_to(thr, (tq, LANES))

    tie = done == 2

    @pl.when(jnp.max(done) == 2)
    def _():
        need = topk - cnt_hi
        thr_b = jnp.broadcast_to(thr, (tq, LANES))
        lane_i = lax.broadcasted_iota(jnp.int32, (tq, LANES), 1)

        def count_eq_le(jmax):
            j_b = jnp.broadcast_to(jmax, (tq, LANES))

            def body(c, cnt):
                k0 = pl.multiple_of(c * LANES, LANES)
                kc = key_ref[:, pl.ds(k0, LANES)]
                return cnt + jnp.where((kc == thr_b) & (lane_i + k0 <= j_b), 1, 0)

            cnt = lax.fori_loop(0, 2 * n_chunks, body, jnp.zeros((tq, LANES), jnp.int32))
            return jnp.sum(cnt, axis=1, keepdims=True)

        def jbody(_, st):
            jlo, jhi = st
            jmid = (jlo + jhi) >> 1
            ok = count_eq_le(jmid) >= need
            return jnp.where(ok, jlo, jmid + 1), jnp.where(ok, jmid, jhi)

        jlo, _ = lax.fori_loop(0, 14, jbody, (jnp.zeros((tq, 1), jnp.int32),
                                               jnp.broadcast_to(t0 + tq - 1, (tq, 1)).astype(jnp.int32)))
        j_b = jnp.broadcast_to(jnp.where(tie, jlo, 0x7FFFFFFF), (tq, LANES))

        def demote(c, carry):
            k0 = pl.multiple_of(c * LANES, LANES)
            kc = key_ref[:, pl.ds(k0, LANES)]
            key_ref[:, pl.ds(k0, LANES)] = jnp.where((kc == thr_b) & (lane_i + k0 > j_b), kc - 1, kc)
            return carry

        lax.fori_loop(0, 2 * n_chunks, demote, 0)

    m_ref[...] = jnp.full(m_ref.shape, MASK_VALUE, F32)
    l_ref[...] = jnp.zeros(l_ref.shape, F32)
    acc_ref[...] = jnp.zeros(acc_ref.shape, F32)
    lane = lax.broadcasted_iota(jnp.int32, (1, LANES), 1)
    low = lane < HEAD_DIM
    reps = P3_CHUNK // LANES

    def p3(c, carry):
        k0 = pl.multiple_of(c * P3_CHUNK, P3_CHUNK)
        thr_b = thr_ref[...]
        sel = key_ref[:, pl.ds(k0, P3_CHUNK)] >= jnp.concatenate([thr_b] * reps, axis=1)
        for j in range(N_PAIRS):
            sl = slice(j * LANES, (j + 1) * LANES)
            kc = kb_ref[pl.ds(k0, P3_CHUNK), sl]
            vc = vb_ref[pl.ds(k0, P3_CHUNK), sl]
            alphas, pvs = [], []
            for hb in range(2):
                h = 2 * j + hb
                s = jnp.where(sel, _dot_t(qbm_ref[h], kc), MASK_VALUE)
                m_old = m_ref[h]
                m_new = jnp.maximum(m_old, jnp.max(s, axis=1, keepdims=True))
                alpha = jnp.exp(m_old - m_new)
                p = jnp.exp(s - jnp.concatenate([m_new] * reps, axis=1))
                l_ref[h] = alpha * l_ref[h] + jnp.sum(p, axis=1, keepdims=True)
                m_ref[h] = m_new
                alphas.append(alpha)
                pvs.append(_dot(p.astype(BF16), vc))
            acc_ref[:, sl] = (jnp.where(low, alphas[0], alphas[1]) * acc_ref[:, sl]
                              + jnp.where(low, pvs[0], pvs[1]))
        return carry

    lax.fori_loop(0, n_chunks, p3, 0)
    for j in range(N_PAIRS):
        sl = slice(j * LANES, (j + 1) * LANES)
        l_pair = jnp.where(low, l_ref[2 * j], l_ref[2 * j + 1])
        o_ref[:, sl] = (acc_ref[:, sl] / l_pair).astype(BF16)


def _sparse_attention(zb, zi, wi, b, t, topk):
    tq = Q_BLOCK
    zb3 = zb.reshape(b, t, 3 * GROUP_W)
    zi3 = zi.reshape(b, t, GROUP_W + LANES)
    wi3 = wi.reshape(b, t, LANES)
    qmap = lambda bi, qi: (bi, qi, 0)
    out = pl.pallas_call(
        functools.partial(_dsa_kernel, topk=topk),
        grid=(b, t // tq),
        in_specs=[pl.BlockSpec((None, tq, GROUP_W), qmap),
                  pl.BlockSpec((None, tq, GROUP_W), qmap),
                  pl.BlockSpec((None, tq, LANES), qmap),
                  pl.BlockSpec((None, t, GROUP_W), lambda bi, qi: (bi, 0, 1)),
                  pl.BlockSpec((None, t, GROUP_W), lambda bi, qi: (bi, 0, 2)),
                  pl.BlockSpec((None, t, LANES), lambda bi, qi: (bi, 0, GROUP_W // LANES))],
        out_specs=pl.BlockSpec((None, tq, GROUP_W), qmap),
        out_shape=jax.ShapeDtypeStruct((b, t, GROUP_W), BF16),
        scratch_shapes=[pltpu.VMEM((tq, t), jnp.int32),
                        pltpu.VMEM((N_HEADS, tq, LANES), BF16),
                        pltpu.VMEM((N_HEADS, tq, LANES), BF16),
                        pltpu.VMEM((N_HEADS, tq, LANES), F32),
                        pltpu.VMEM((N_HEADS, tq, LANES), F32),
                        pltpu.VMEM((N_HEADS, tq, LANES), F32),
                        pltpu.VMEM((tq, GROUP_W), F32),
                        pltpu.VMEM((tq, LANES), jnp.int32)],
        compiler_params=_cparams(2), name="dsa_attn",
    )(zb3, zi3, wi3, zb3, zb3, zi3)
    return out.reshape(b * t, GROUP_W)


def _out_proj_kernel(x_ref, o1_ref, o2_ref, o3_ref, l1_ref, l2_ref, l3_ref, ob_ref, wo_ref, y_ref):
    l1, l2, l3 = l1_ref[...], l2_ref[...], l3_ref[...]
    m = jnp.maximum(jnp.maximum(l1, l2), l3)
    e1, e2, e3 = jnp.exp(l1 - m), jnp.exp(l2 - m), jnp.exp(l3 - m)
    out_a = (e1 * o1_ref[...] + e2 * o2_ref[...] + e3 * o3_ref[...]) / (e1 + e2 + e3)
    y = x_ref[...] + _dot(out_a.astype(BF16), wo_ref[0:GROUP_W, :])
    y_ref[...] = y + _dot(ob_ref[...], wo_ref[GROUP_W:2 * GROUP_W, :])


def _out_proj(x2, branches, ob, w_o, tm):
    n, d = x2.shape
    row = lambda i: (i, 0)
    gspec = pl.BlockSpec((tm, GROUP_W), row)
    (o1, l1), (o2, l2), (o3, l3) = branches
    return pl.pallas_call(
        _out_proj_kernel,
        grid=(n // tm,),
        in_specs=[pl.BlockSpec((tm, d), row)] + [gspec] * 7 + [pl.BlockSpec((2 * GROUP_W, d), lambda i: (0, 0))],
        out_specs=pl.BlockSpec((tm, d), row),
        out_shape=jax.ShapeDtypeStruct((n, d), F32),
        compiler_params=_cparams(1), name="merge_out_proj",
    )(x2, o1, o2, o3, l1, l2, l3, ob, w_o.astype(BF16))


HALO = 8


def _ffn_kernel(x_ref, g_ref, wup_ref, cw_ref, cb_ref, wdn_ref, y_ref,
                acc_ref, buf_ref, tail_ref, *, tiles_per_seq):
    tm = x_ref.shape[0]
    i = pl.program_id(0)

    @pl.when(i % tiles_per_seq == 0)
    def _():
        tail_ref[...] = jnp.zeros(tail_ref.shape, F32)

    x = x_ref[...]
    h = _rms(x, g_ref[...]).astype(BF16)
    acc_ref[...] = x

    def conv(c):
        u = _dot(h, wup_ref[c])
        buf_ref[0:HALO, :] = tail_ref[c]
        buf_ref[HALO:HALO + tm, :] = u
        tail_ref[c] = u[tm - HALO:, :]
        w = cw_ref[c]
        y = cb_ref[c] + w[2:3, :] * u
        y = y + w[1:2, :] * buf_ref[HALO - 1:HALO - 1 + tm, :]
        return y + w[0:1, :] * buf_ref[HALO - 2:HALO - 2 + tm, :]

    def body(c, carry):
        gate = conv(c)
        up = conv(c + N_FF_CHUNKS)
        act = (gate * jax.nn.sigmoid(gate) * up).astype(BF16)
        acc_ref[...] += _dot(act, wdn_ref[c])
        return carry

    lax.fori_loop(0, N_FF_CHUNKS, body, 0)
    y_ref[...] = acc_ref[...]


def _ffn(x2, t, ffn_norm, w_up, conv_w, conv_b, w_down, tm):
    n, d = x2.shape
    nc2 = 2 * N_FF_CHUNKS
    wup = w_up.reshape(d, nc2, FF_CHUNK).transpose(1, 0, 2).astype(BF16)
    cw = conv_w.reshape(CONV_WIDTH, nc2, FF_CHUNK).transpose(1, 0, 2)
    cb = conv_b.reshape(nc2, 1, FF_CHUNK)
    wdn = w_down.reshape(N_FF_CHUNKS, FF_CHUNK, d).astype(BF16)
    row = lambda i: (i, 0)
    c3 = lambda i: (0, 0, 0)
    return pl.pallas_call(
        functools.partial(_ffn_kernel, tiles_per_seq=t // tm),
        grid=(n // tm,),
        in_specs=[pl.BlockSpec((tm, d), row), pl.BlockSpec((1, d), lambda i: (0, 0)),
                  pl.BlockSpec((nc2, d, FF_CHUNK), c3), pl.BlockSpec((nc2, CONV_WIDTH, FF_CHUNK), c3),
                  pl.BlockSpec((nc2, 1, FF_CHUNK), c3), pl.BlockSpec((N_FF_CHUNKS, FF_CHUNK, d), c3)],
        out_specs=pl.BlockSpec((tm, d), row),
        out_shape=jax.ShapeDtypeStruct((n, d), F32),
        scratch_shapes=[pltpu.VMEM((tm, d), F32), pltpu.VMEM((tm + HALO, FF_CHUNK), F32),
                        pltpu.VMEM((nc2, HALO, FF_CHUNK), F32)],
        compiler_params=_cparams(1), name="conv_ffn",
    )(x2, ffn_norm.reshape(1, d), wup, cw, cb, wdn)


def _ple_kernel(x_ref, p_ref, g_ref, wg_ref, wp_ref, gf_ref, y_ref, *, final):
    x = x_ref[...]
    gate = jax.nn.sigmoid(_dot(_rms(x, g_ref[...]).astype(BF16), wg_ref[...]))
    x = x + gate * _dot(p_ref[...].astype(BF16), wp_ref[...])
    y_ref[...] = _rms(x, gf_ref[...]) if final else x


def _ple_final(x2, p2, ple_norm, w_gate, w_proj, final_norm, final, tm):
    n, d = x2.shape
    pd = p2.shape[1]
    row = lambda i: (i, 0)
    const = lambda i: (0, 0)
    return pl.pallas_call(
        functools.partial(_ple_kernel, final=final),
        grid=(n // tm,),
        in_specs=[pl.BlockSpec((tm, d), row), pl.BlockSpec((tm, pd), row), pl.BlockSpec((1, d), const),
                  pl.BlockSpec((d, d), const), pl.BlockSpec((pd, d), const), pl.BlockSpec((1, d), const)],
        out_specs=pl.BlockSpec((tm, d), row),
        out_shape=jax.ShapeDtypeStruct((n, d), F32),
        compiler_params=_cparams(1), name="ple_final_norm",
    )(x2, p2, ple_norm.reshape(1, d), w_gate.astype(BF16), w_proj.astype(BF16), final_norm.reshape(1, d))


def kernel(x, p, positions, attn_norm, w_in, w_o, ffn_norm, w_up, conv_w, conv_b, w_down,
           ple_norm, w_ple_gate, w_ple_proj, final_norm):
    b, t, d = x.shape
    depth = p.shape[0]
    n = b * t
    topk = min(TOPK_MAX, t // 4)
    assert t % DILATED_PAIRS[-1][0] == 0 and t % Q_BLOCK == 0
    tm = 256
    x2 = x.reshape(n, d)
    pos2 = positions.reshape(n, 1)
    for i in range(depth):
        za, zb, zi, wi = _project(x2, pos2, attn_norm[i], w_in[i], tm)
        branches = [_dilated_branch(za, b, t, dil) for _, dil in DILATED_PAIRS]
        ob = _sparse_attention(zb, zi, wi, b, t, topk)
        x2 = _out_proj(x2, branches, ob, w_o[i], tm)
        x2 = _ffn(x2, t, ffn_norm[i], w_up[i], conv_w[i], conv_b[i], w_down[i], tm)
        x2 = _ple_final(x2, p[i].reshape(n, -1), ple_norm[i], w_ple_gate[i], w_ple_proj[i],
                        final_norm, i == depth - 1, tm)
    return x2.reshape(b, t, d)
```

```python
import functools
import math

import numpy as np
import jax
import jax.numpy as jnp
from jax import lax
from jax.experimental import pallas as pl
from jax.experimental.pallas import tpu as pltpu

F32 = jnp.float32
BF16 = jnp.bfloat16

HEAD_DIM = 64
HALF = HEAD_DIM // 2
N_HEADS = 8
GROUP_W = N_HEADS * HEAD_DIM
N_PAIRS = N_HEADS // 2
LANES = 128
DILATED_PAIRS = ((128, 1), (512, 4), (2048, 16))
WIN_STEPS = 128
TOPK_MAX = 256
D_FF = 2816
FF_CHUNK = 256
N_FF_CHUNKS = D_FF // FF_CHUNK
CONV_WIDTH = 3
ROPE_THETA = 10000.0
RMS_EPS = 1e-6
MASK_VALUE = -1e30
KEY_NEG_INF = -2139095041
VMEM_LIMIT = 56 * 1024 * 1024


def _cparams(n_axes):
    return pltpu.CompilerParams(dimension_semantics=("arbitrary",) * n_axes,
                                vmem_limit_bytes=VMEM_LIMIT)


def _dot_t(a, b):
    return lax.dot_general(a, b, (((1,), (1,)), ((), ())), preferred_element_type=F32)


def _dot(a, b):
    return jnp.dot(a, b, preferred_element_type=F32)


def _rms(x, g):
    ms = jnp.mean(x * x, axis=-1, keepdims=True)
    return (x * lax.rsqrt(ms + RMS_EPS)) * g


def _pair_perm():
    idx = []
    for j in range(N_PAIRS):
        a, b = 2 * j * HEAD_DIM, (2 * j + 1) * HEAD_DIM
        idx += list(range(a, a + HALF)) + list(range(b, b + HALF))
        idx += list(range(a + HALF, a + HEAD_DIM)) + list(range(b + HALF, b + HEAD_DIM))
    return np.asarray(idx, np.int32)


def _in_proj_columns():
    perm = _pair_perm()
    nat = np.arange(GROUP_W, dtype=np.int32)
    cols = []
    for g, roped in enumerate((True, True, False, True, True, False, True)):
        cols.append(g * GROUP_W + (perm if roped else nat))
    kbase = 7 * GROUP_W
    k1 = kbase + np.arange(HALF, dtype=np.int32)
    k2 = kbase + HALF + np.arange(HALF, dtype=np.int32)
    cols.append(np.concatenate([k1, k1, k2, k2]))
    return np.concatenate(cols)


def _proj_kernel(x_ref, pos_ref, g_ref, invf_ref, w_ref, wiw_ref, za_ref, zb_ref, zi_ref, wi_ref):
    x = x_ref[...]
    h32 = _rms(x, g_ref[...])
    h = h32.astype(BF16)
    ang = pos_ref[...].astype(F32) * invf_ref[...]
    lane = lax.broadcasted_iota(jnp.int32, (1, LANES), 1)
    cos = jnp.cos(ang)
    sin = jnp.sin(ang) * jnp.where(lane < 2 * HALF, -1.0, 1.0)

    def rope(z):
        return z * cos + pltpu.roll(z, 2 * HALF, 1) * sin

    q_scale = HEAD_DIM ** -0.5
    plan = ((za_ref, 0, True, q_scale), (za_ref, 1, True, 1.0), (za_ref, 2, False, 1.0),
            (zb_ref, 0, True, q_scale * math.log2(math.e)), (zb_ref, 1, True, 1.0), (zb_ref, 2, False, 1.0),
            (zi_ref, 0, True, q_scale))
    for g, (out_ref, slot, roped, scale) in enumerate(plan):
        z = _dot(h, w_ref[:, g * GROUP_W:(g + 1) * GROUP_W])
        for j in range(N_PAIRS):
            zj = z[:, j * LANES:(j + 1) * LANES]
            if roped:
                zj = rope(zj)
            if scale != 1.0:
                zj = zj * scale
            out_ref[:, slot * GROUP_W + j * LANES: slot * GROUP_W + (j + 1) * LANES] = zj.astype(BF16)
    zk = _dot(h, w_ref[:, 7 * GROUP_W: 7 * GROUP_W + LANES])
    zi_ref[:, GROUP_W:GROUP_W + LANES] = rope(zk).astype(BF16)
    wi_ref[...] = _dot(h, wiw_ref[...]) * (N_HEADS ** -0.5)


def _project(x2, pos2, attn_norm, w_in, tm):
    n, d = x2.shape
    cols = _in_proj_columns()
    w_main = jnp.take(w_in, cols, axis=1).astype(BF16)
    w_wi = jnp.pad(w_in[:, 7 * GROUP_W + HEAD_DIM:], ((0, 0), (0, LANES - N_HEADS))).astype(BF16)
    inv_freq = 1.0 / (ROPE_THETA ** (np.arange(0, HEAD_DIM, 2, dtype=np.float32) / HEAD_DIM))
    invf = jnp.asarray(np.tile(inv_freq, 4)[None, :], F32)
    wcols = w_main.shape[1]
    const = lambda i: (0, 0)
    row = lambda i: (i, 0)
    return pl.pallas_call(
        _proj_kernel,
        grid=(n // tm,),
        in_specs=[pl.BlockSpec((tm, d), row), pl.BlockSpec((tm, 1), row),
                  pl.BlockSpec((1, d), const), pl.BlockSpec((1, LANES), const),
                  pl.BlockSpec((d, wcols), const), pl.BlockSpec((d, LANES), const)],
        out_specs=[pl.BlockSpec((tm, 3 * GROUP_W), row), pl.BlockSpec((tm, 3 * GROUP_W), row),
                   pl.BlockSpec((tm, GROUP_W + LANES), row), pl.BlockSpec((tm, LANES), row)],
        out_shape=[jax.ShapeDtypeStruct((n, 3 * GROUP_W), BF16),
                   jax.ShapeDtypeStruct((n, 3 * GROUP_W), BF16),
                   jax.ShapeDtypeStruct((n, GROUP_W + LANES), BF16),
                   jax.ShapeDtypeStruct((n, LANES), F32)],
        compiler_params=_cparams(1), name="in_proj_rope",
    )(x2, pos2, attn_norm.reshape(1, d), invf, w_main, w_wi)


def _head_lane_masks():
    lane = lax.broadcasted_iota(jnp.int32, (1, LANES), 1)
    first = (lane % HEAD_DIM) < HALF
    return first, jnp.logical_not(first)


def _dilated_kernel(q_ref, kc_ref, kp_ref, vc_ref, vp_ref, o_ref, lse_ref):
    nb = pl.program_id(2)
    n = WIN_STEPS
    a = lax.broadcasted_iota(jnp.int32, (n, 2 * n), 0)
    c = lax.broadcasted_iota(jnp.int32, (n, 2 * n), 1)
    band = (c >= a) & (c <= a + n) & ((nb > 0) | (c >= n))
    masks = _head_lane_masks()
    lane = lax.broadcasted_iota(jnp.int32, (1, LANES), 1)
    low = lane < HEAD_DIM
    for j in range(N_PAIRS):
        sl = slice(j * LANES, (j + 1) * LANES)
        q = q_ref[:, sl]
        kk = jnp.concatenate([kp_ref[:, sl], kc_ref[:, sl]], axis=0)
        vv = jnp.concatenate([vp_ref[:, sl], vc_ref[:, sl]], axis=0)
        outs, lses = [], []
        for hb in range(2):
            qm = jnp.where(masks[hb], q, jnp.zeros_like(q))
            s = _dot_t(qm, kk)
            s = jnp.where(band, s, -jnp.inf)
            m = jnp.max(s, axis=1, keepdims=True)
            p = jnp.exp(s - m)
            l = jnp.sum(p, axis=1, keepdims=True)
            outs.append(_dot(p.astype(BF16), vv) / l)
            lses.append(m + jnp.log(l))
        o_ref[:, sl] = jnp.where(low, outs[0], outs[1])
        lse_ref[:, sl] = jnp.where(low, lses[0], lses[1])


def _dilated_branch(za, b, t, dil):
    n = WIN_STEPS
    rows = t // dil
    zv = za.reshape(b, rows, dil * 3 * GROUP_W)
    blk = (None, n, GROUP_W)
    cur = lambda slot: (lambda bi, r, nb: (bi, nb, r * 3 + slot))
    prev = lambda slot: (lambda bi, r, nb: (bi, jnp.maximum(nb - 1, 0), r * 3 + slot))
    out_map = lambda bi, r, nb: (bi, nb, r)
    o, lse = pl.pallas_call(
        _dilated_kernel,
        grid=(b, dil, rows // n),
        in_specs=[pl.BlockSpec(blk, cur(0)), pl.BlockSpec(blk, cur(1)), pl.BlockSpec(blk, prev(1)),
                  pl.BlockSpec(blk, cur(2)), pl.BlockSpec(blk, prev(2))],
        out_specs=[pl.BlockSpec(blk, out_map), pl.BlockSpec(blk, out_map)],
        out_shape=[jax.ShapeDtypeStruct((b, rows, dil * GROUP_W), F32)] * 2,
        compiler_params=_cparams(3), name=f"dilated_attn_d{dil}",
    )(zv, zv, zv, zv, zv)
    return o.reshape(b * t, GROUP_W), lse.reshape(b * t, GROUP_W)


DSA_TQ = 256
P1_CHUNK = 256
P3_CHUNK = 256
CNT_ROWS = 32


def _to_key(x):
    bits = pltpu.bitcast(x, jnp.int32)
    return bits ^ ((bits >> 31) & 0x7FFFFFFF)


def _dsa_kernel(qb_ref, qi_ref, wit_ref, kb_ref, vt_ref, ki_ref, ot_ref,
                key_ref, qim_ref, qbm_ref, m_ref, l_ref, acc_ref, thr_ref, s_ref, *, topk):
    tq = DSA_TQ
    qblk = pl.program_id(1)
    t0 = qblk * tq
    n1 = qblk + 1
    masks = _head_lane_masks()

    for h in range(N_HEADS):
        j, hb = divmod(h, 2)
        sl = slice(j * LANES, (j + 1) * LANES)
        qi = qi_ref[:, sl]
        qb = qb_ref[:, sl]
        qim_ref[h] = jnp.where(masks[hb], qi, jnp.zeros_like(qi))
        qbm_ref[h] = jnp.where(masks[hb], qb, jnp.zeros_like(qb))

    def p1_chunk(c, diagonal):
        k0 = pl.multiple_of(c * P1_CHUNK, P1_CHUNK)
        kic = ki_ref[pl.ds(k0, P1_CHUNK), :]
        acc = jnp.zeros((P1_CHUNK, tq), F32)
        for h in range(N_HEADS):
            acc = acc + wit_ref[h:h + 1, :] * jnp.maximum(_dot_t(kic, qim_ref[h]), 0.0)
        if diagonal:
            kpos = lax.broadcasted_iota(jnp.int32, (P1_CHUNK, tq), 0)
            qpos = lax.broadcasted_iota(jnp.int32, (P1_CHUNK, tq), 1)
            acc = jnp.where(kpos <= qpos, acc, -jnp.inf)
        key_ref[pl.ds(k0, P1_CHUNK), :] = _to_key(acc)

    def p1(c, carry):
        p1_chunk(c, False)
        return carry

    lax.fori_loop(0, qblk, p1, 0)
    p1_chunk(qblk, True)

    def count_ge(thr):
        def body(c, cnt):
            k0 = pl.multiple_of(c * P1_CHUNK, P1_CHUNK)
            x = jnp.where(key_ref[pl.ds(k0, P1_CHUNK), :] >= thr, 1, 0)
            return cnt + jnp.sum(x.reshape(P1_CHUNK // CNT_ROWS, CNT_ROWS, tq), axis=0)

        cnt = lax.fori_loop(0, n1, body, jnp.zeros((CNT_ROWS, tq), jnp.int32))
        return jnp.sum(cnt, axis=0, keepdims=True)

    t_row = t0 + lax.broadcasted_iota(jnp.int32, (1, tq), 1)
    all_sel = t_row + 1 <= topk
    lo0 = jnp.full((1, tq), KEY_NEG_INF + 1, jnp.int32)
    hi0 = jnp.full((1, tq), 0x7FFFFFFF, jnp.int32)
    done0 = all_sel.astype(jnp.int32)
    cnt_hi0 = jnp.zeros((1, tq), jnp.int32)

    def bis_cond(st):
        return jnp.min(st[3]) == 0

    def bis_body(st):
        lo, hi, thr, done, cnt_hi = st
        mid = (lo >> 1) + (hi >> 1) + (((lo & 1) + (hi & 1) + 1) >> 1)
        c = count_ge(mid)
        ge = c >= topk
        active = done == 0
        hit = active & (c == topk)
        new_lo = jnp.where(active & ge, mid, lo)
        new_hi = jnp.where(active & jnp.logical_not(ge), mid - 1, hi)
        new_cnt_hi = jnp.where(active & jnp.logical_not(ge), c, cnt_hi)
        conv = active & jnp.logical_not(hit) & (new_lo == new_hi)
        new_thr = jnp.where(hit, mid, jnp.where(conv, new_lo, thr))
        new_done = jnp.where(hit, 1, jnp.where(conv, 2, done))
        return new_lo, new_hi, new_thr, new_done, new_cnt_hi

    _, _, thr, done, cnt_hi = lax.while_loop(bis_cond, bis_body, (lo0, hi0, lo0, done0, cnt_hi0))
    thr_ref[...] = thr

    tie = done == 2

    @pl.when(jnp.max(done) == 2)
    def _():
        need = topk - cnt_hi
        sub_i = lax.broadcasted_iota(jnp.int32, (P1_CHUNK, tq), 0)

        def count_eq_le(jmax):
            def body(c, cnt):
                k0 = pl.multiple_of(c * P1_CHUNK, P1_CHUNK)
                kc = key_ref[pl.ds(k0, P1_CHUNK), :]
                x = jnp.where((kc == thr) & (sub_i + k0 <= jmax), 1, 0)
                return cnt + jnp.sum(x.reshape(P1_CHUNK // CNT_ROWS, CNT_ROWS, tq), axis=0)

            cnt = lax.fori_loop(0, n1, body, jnp.zeros((CNT_ROWS, tq), jnp.int32))
            return jnp.sum(cnt, axis=0, keepdims=True)

        def jbody(_, st):
            jlo, jhi = st
            jmid = (jlo + jhi) >> 1
            ok = count_eq_le(jmid) >= need
            return jnp.where(ok, jlo, jmid + 1), jnp.where(ok, jmid, jhi)

        jlo, _ = lax.fori_loop(0, 14, jbody, (jnp.zeros((1, tq), jnp.int32),
                                               jnp.broadcast_to(t0 + tq - 1, (1, tq)).astype(jnp.int32)))
        jcut = jnp.where(tie, jlo, 0x7FFFFFFF)

        def demote(c, carry):
            k0 = pl.multiple_of(c * P1_CHUNK, P1_CHUNK)
            kc = key_ref[pl.ds(k0, P1_CHUNK), :]
            key_ref[pl.ds(k0, P1_CHUNK), :] = jnp.where((kc == thr) & (sub_i + k0 > jcut), kc - 1, kc)
            return carry

        lax.fori_loop(0, n1, demote, 0)

    m_ref[...] = jnp.full(m_ref.shape, MASK_VALUE, F32)
    l_ref[...] = jnp.zeros(l_ref.shape, F32)
    acc_ref[...] = jnp.zeros(acc_ref.shape, F32)

    def p3(c, carry):
        k0 = pl.multiple_of(c * P3_CHUNK, P3_CHUNK)
        sel = key_ref[pl.ds(k0, P3_CHUNK), :] >= thr_ref[...]
        chunk_max = []
        for h in range(N_HEADS):
            j = h // 2
            kc = kb_ref[pl.ds(k0, P3_CHUNK), j * LANES:(j + 1) * LANES]
            s = jnp.where(sel, _dot_t(kc, qbm_ref[h]), MASK_VALUE)
            s_ref[h] = s
            chunk_max.append(jnp.max(s, axis=0, keepdims=True))
        for h in range(N_HEADS):
            rows = slice(h * HEAD_DIM, (h + 1) * HEAD_DIM)
            m_old = m_ref[h]
            m_new = jnp.maximum(m_old, chunk_max[h])
            alpha = jnp.exp2(m_old - m_new)
            p = jnp.exp2(s_ref[h] - m_new)
            l_ref[h] = alpha * l_ref[h] + jnp.sum(p, axis=0, keepdims=True)
            m_ref[h] = m_new
            pv = _dot(vt_ref[rows, pl.ds(k0, P3_CHUNK)], p.astype(BF16))
            acc_ref[rows, :] = alpha * acc_ref[rows, :] + pv
        return carry

    lax.fori_loop(0, n1 * (P1_CHUNK // P3_CHUNK), p3, 0)
    for h in range(N_HEADS):
        rows = slice(h * HEAD_DIM, (h + 1) * HEAD_DIM)
        ot_ref[rows, :] = (acc_ref[rows, :] / l_ref[h]).astype(BF16)


def _sparse_attention(zb, zi, wi, b, t, topk):
    tq = DSA_TQ
    zb3 = zb.reshape(b, t, 3 * GROUP_W)
    zi3 = zi.reshape(b, t, GROUP_W + LANES)
    vt = zb3[:, :, 2 * GROUP_W:].transpose(0, 2, 1)
    wit = wi.reshape(b, t, LANES)[:, :, :N_HEADS].transpose(0, 2, 1)
    qmap = lambda bi, qi: (bi, qi, 0)
    tmap = lambda bi, qi: (bi, 0, qi)
    out_t = pl.pallas_call(
        functools.partial(_dsa_kernel, topk=topk),
        grid=(b, t // tq),
        in_specs=[pl.BlockSpec((None, tq, GROUP_W), qmap),
                  pl.BlockSpec((None, tq, GROUP_W), qmap),
                  pl.BlockSpec((None, N_HEADS, tq), tmap),
                  pl.BlockSpec((None, t, GROUP_W), lambda bi, qi: (bi, 0, 1)),
                  pl.BlockSpec((None, GROUP_W, t), lambda bi, qi: (bi, 0, 0)),
                  pl.BlockSpec((None, t, LANES), lambda bi, qi: (bi, 0, GROUP_W // LANES))],
        out_specs=pl.BlockSpec((None, GROUP_W, tq), tmap),
        out_shape=jax.ShapeDtypeStruct((b, GROUP_W, t), BF16),
        scratch_shapes=[pltpu.VMEM((t, tq), jnp.int32),
                        pltpu.VMEM((N_HEADS, tq, LANES), BF16),
                        pltpu.VMEM((N_HEADS, tq, LANES), BF16),
                        pltpu.VMEM((N_HEADS, 1, tq), F32),
                        pltpu.VMEM((N_HEADS, 1, tq), F32),
                        pltpu.VMEM((GROUP_W, tq), F32),
                        pltpu.VMEM((1, tq), jnp.int32),
                        pltpu.VMEM((N_HEADS, P3_CHUNK, tq), F32)],
        compiler_params=_cparams(2), name="dsa_attn",
    )(zb3, zi3, wit, zb3, vt, zi3)
    return out_t.transpose(0, 2, 1).reshape(b * t, GROUP_W)


def _out_proj_kernel(x_ref, o1_ref, o2_ref, o3_ref, l1_ref, l2_ref, l3_ref, ob_ref, wo_ref, y_ref):
    l1, l2, l3 = l1_ref[...], l2_ref[...], l3_ref[...]
    m = jnp.maximum(jnp.maximum(l1, l2), l3)
    e1, e2, e3 = jnp.exp(l1 - m), jnp.exp(l2 - m), jnp.exp(l3 - m)
    out_a = (e1 * o1_ref[...] + e2 * o2_ref[...] + e3 * o3_ref[...]) / (e1 + e2 + e3)
    y = x_ref[...] + _dot(out_a.astype(BF16), wo_ref[0:GROUP_W, :])
    y_ref[...] = y + _dot(ob_ref[...], wo_ref[GROUP_W:2 * GROUP_W, :])


def _out_proj(x2, branches, ob, w_o, tm):
    n, d = x2.shape
    row = lambda i: (i, 0)
    gspec = pl.BlockSpec((tm, GROUP_W), row)
    (o1, l1), (o2, l2), (o3, l3) = branches
    return pl.pallas_call(
        _out_proj_kernel,
        grid=(n // tm,),
        in_specs=[pl.BlockSpec((tm, d), row)] + [gspec] * 7 + [pl.BlockSpec((2 * GROUP_W, d), lambda i: (0, 0))],
        out_specs=pl.BlockSpec((tm, d), row),
        out_shape=jax.ShapeDtypeStruct((n, d), F32),
        compiler_params=_cparams(1), name="merge_out_proj",
    )(x2, o1, o2, o3, l1, l2, l3, ob, w_o.astype(BF16))


HALO = 8


def _ffn_kernel(x_ref, g_ref, wup_ref, cw_ref, cb_ref, wdn_ref, y_ref,
                acc_ref, buf_ref, tail_ref, *, tiles_per_seq):
    tm = x_ref.shape[0]
    i = pl.program_id(0)

    @pl.when(i % tiles_per_seq == 0)
    def _():
        tail_ref[...] = jnp.zeros(tail_ref.shape, F32)

    x = x_ref[...]
    h = _rms(x, g_ref[...]).astype(BF16)
    acc_ref[...] = x

    def conv(c):
        u = _dot(h, wup_ref[c])
        buf_ref[0:HALO, :] = tail_ref[c]
        buf_ref[HALO:HALO + tm, :] = u
        tail_ref[c] = u[tm - HALO:, :]
        w = cw_ref[c]
        y = cb_ref[c] + w[2:3, :] * u
        y = y + w[1:2, :] * buf_ref[HALO - 1:HALO - 1 + tm, :]
        return y + w[0:1, :] * buf_ref[HALO - 2:HALO - 2 + tm, :]

    def body(c, carry):
        gate = conv(c)
        up = conv(c + N_FF_CHUNKS)
        act = (gate * jax.nn.sigmoid(gate) * up).astype(BF16)
        acc_ref[...] += _dot(act, wdn_ref[c])
        return carry

    lax.fori_loop(0, N_FF_CHUNKS, body, 0)
    y_ref[...] = acc_ref[...]


def _ffn(x2, t, ffn_norm, w_up, conv_w, conv_b, w_down, tm):
    n, d = x2.shape
    nc2 = 2 * N_FF_CHUNKS
    wup = w_up.reshape(d, nc2, FF_CHUNK).transpose(1, 0, 2).astype(BF16)
    cw = conv_w.reshape(CONV_WIDTH, nc2, FF_CHUNK).transpose(1, 0, 2)
    cb = conv_b.reshape(nc2, 1, FF_CHUNK)
    wdn = w_down.reshape(N_FF_CHUNKS, FF_CHUNK, d).astype(BF16)
    row = lambda i: (i, 0)
    c3 = lambda i: (0, 0, 0)
    return pl.pallas_call(
        functools.partial(_ffn_kernel, tiles_per_seq=t // tm),
        grid=(n // tm,),
        in_specs=[pl.BlockSpec((tm, d), row), pl.BlockSpec((1, d), lambda i: (0, 0)),
                  pl.BlockSpec((nc2, d, FF_CHUNK), c3), pl.BlockSpec((nc2, CONV_WIDTH, FF_CHUNK), c3),
                  pl.BlockSpec((nc2, 1, FF_CHUNK), c3), pl.BlockSpec((N_FF_CHUNKS, FF_CHUNK, d), c3)],
        out_specs=pl.BlockSpec((tm, d), row),
        out_shape=jax.ShapeDtypeStruct((n, d), F32),
        scratch_shapes=[pltpu.VMEM((tm, d), F32), pltpu.VMEM((tm + HALO, FF_CHUNK), F32),
                        pltpu.VMEM((nc2, HALO, FF_CHUNK), F32)],
        compiler_params=_cparams(1), name="conv_ffn",
    )(x2, ffn_norm.reshape(1, d), wup, cw, cb, wdn)


def _ple_kernel(x_ref, p_ref, g_ref, wg_ref, wp_ref, gf_ref, y_ref, *, final):
    x = x_ref[...]
    gate = jax.nn.sigmoid(_dot(_rms(x, g_ref[...]).astype(BF16), wg_ref[...]))
    x = x + gate * _dot(p_ref[...].astype(BF16), wp_ref[...])
    y_ref[...] = _rms(x, gf_ref[...]) if final else x


def _ple_final(x2, p2, ple_norm, w_gate, w_proj, final_norm, final, tm):
    n, d = x2.shape
    pd = p2.shape[1]
    row = lambda i: (i, 0)
    const = lambda i: (0, 0)
    return pl.pallas_call(
        functools.partial(_ple_kernel, final=final),
        grid=(n // tm,),
        in_specs=[pl.BlockSpec((tm, d), row), pl.BlockSpec((tm, pd), row), pl.BlockSpec((1, d), const),
                  pl.BlockSpec((d, d), const), pl.BlockSpec((pd, d), const), pl.BlockSpec((1, d), const)],
        out_specs=pl.BlockSpec((tm, d), row),
        out_shape=jax.ShapeDtypeStruct((n, d), F32),
        compiler_params=_cparams(1), name="ple_final_norm",
    )(x2, p2, ple_norm.reshape(1, d), w_gate.astype(BF16), w_proj.astype(BF16), final_norm.reshape(1, d))


def kernel(x, p, positions, attn_norm, w_in, w_o, ffn_norm, w_up, conv_w, conv_b, w_down,
           ple_norm, w_ple_gate, w_ple_proj, final_norm):
    b, t, d = x.shape
    depth = p.shape[0]
    n = b * t
    topk = min(TOPK_MAX, t // 4)
    assert t % DILATED_PAIRS[-1][0] == 0 and t % DSA_TQ == 0
    tm = 256
    x2 = x.reshape(n, d)
    pos2 = positions.reshape(n, 1)
    for i in range(depth):
        za, zb, zi, wi = _project(x2, pos2, attn_norm[i], w_in[i], tm)
        branches = [_dilated_branch(za, b, t, dil) for _, dil in DILATED_PAIRS]
        ob = _sparse_attention(zb, zi, wi, b, t, topk)
        x2 = _out_proj(x2, branches, ob, w_o[i], tm)
        x2 = _ffn(x2, t, ffn_norm[i], w_up[i], conv_w[i], conv_b[i], w_down[i], tm)
        x2 = _ple_final(x2, p[i].reshape(n, -1), ple_norm[i], w_ple_gate[i], w_ple_proj[i],
                        final_norm, i == depth - 1, tm)
    return x2.reshape(b, t, d)
```

```python
import functools
import math

import numpy as np
import jax
import jax.numpy as jnp
from jax import lax
from jax.experimental import pallas as pl
from jax.experimental.pallas import tpu as pltpu

F32 = jnp.float32
BF16 = jnp.bfloat16

HEAD_DIM = 64
HALF = HEAD_DIM // 2
N_HEADS = 8
GROUP_W = N_HEADS * HEAD_DIM
N_PAIRS = N_HEADS // 2
LANES = 128
DILATED_PAIRS = ((128, 1), (512, 4), (2048, 16))
WIN_STEPS = 128
TOPK_MAX = 256
D_FF = 2816
FF_CHUNK = 256
N_FF_CHUNKS = D_FF // FF_CHUNK
CONV_WIDTH = 3
ROPE_THETA = 10000.0
RMS_EPS = 1e-6
MASK_VALUE = -1e30
KEY_NEG_INF = -2139095041
VMEM_LIMIT = 56 * 1024 * 1024


def _cparams(n_axes):
    return pltpu.CompilerParams(dimension_semantics=("arbitrary",) * n_axes,
                                vmem_limit_bytes=VMEM_LIMIT)


def _dot_t(a, b):
    return lax.dot_general(a, b, (((1,), (1,)), ((), ())), preferred_element_type=F32)


def _dot(a, b):
    return jnp.dot(a, b, preferred_element_type=F32)


def _rms(x, g):
    ms = jnp.mean(x * x, axis=-1, keepdims=True)
    return (x * lax.rsqrt(ms + RMS_EPS)) * g


def _pair_perm():
    idx = []
    for j in range(N_PAIRS):
        a, b = 2 * j * HEAD_DIM, (2 * j + 1) * HEAD_DIM
        idx += list(range(a, a + HALF)) + list(range(b, b + HALF))
        idx += list(range(a + HALF, a + HEAD_DIM)) + list(range(b + HALF, b + HEAD_DIM))
    return np.asarray(idx, np.int32)


def _in_proj_columns():
    perm = _pair_perm()
    nat = np.arange(GROUP_W, dtype=np.int32)
    cols = []
    for g, roped in enumerate((True, True, False, True, True, False, True)):
        cols.append(g * GROUP_W + (perm if roped else nat))
    kbase = 7 * GROUP_W
    k1 = kbase + np.arange(HALF, dtype=np.int32)
    k2 = kbase + HALF + np.arange(HALF, dtype=np.int32)
    cols.append(np.concatenate([k1, k1, k2, k2]))
    return np.concatenate(cols)


def _proj_kernel(x_ref, pos_ref, g_ref, invf_ref, w_ref, wiw_ref, za_ref, zb_ref, zi_ref, wi_ref):
    x = x_ref[...]
    h32 = _rms(x, g_ref[...])
    h = h32.astype(BF16)
    ang = pos_ref[...].astype(F32) * invf_ref[...]
    lane = lax.broadcasted_iota(jnp.int32, (1, LANES), 1)
    cos = jnp.cos(ang)
    sin = jnp.sin(ang) * jnp.where(lane < 2 * HALF, -1.0, 1.0)

    def rope(z):
        return z * cos + pltpu.roll(z, 2 * HALF, 1) * sin

    q_scale = HEAD_DIM ** -0.5
    plan = ((za_ref, 0, True, q_scale), (za_ref, 1, True, 1.0), (za_ref, 2, False, 1.0),
            (zb_ref, 0, True, q_scale * math.log2(math.e)), (zb_ref, 1, True, 1.0), (zb_ref, 2, False, 1.0),
            (zi_ref, 0, True, q_scale))
    for g, (out_ref, slot, roped, scale) in enumerate(plan):
        z = _dot(h, w_ref[:, g * GROUP_W:(g + 1) * GROUP_W])
        for j in range(N_PAIRS):
            zj = z[:, j * LANES:(j + 1) * LANES]
            if roped:
                zj = rope(zj)
            if scale != 1.0:
                zj = zj * scale
            out_ref[:, slot * GROUP_W + j * LANES: slot * GROUP_W + (j + 1) * LANES] = zj.astype(BF16)
    zk = _dot(h, w_ref[:, 7 * GROUP_W: 7 * GROUP_W + LANES])
    zi_ref[:, GROUP_W:GROUP_W + LANES] = rope(zk).astype(BF16)
    wi_ref[...] = _dot(h, wiw_ref[...]) * (N_HEADS ** -0.5)


def _project(x2, pos2, attn_norm, w_in, tm):
    n, d = x2.shape
    cols = _in_proj_columns()
    w_main = jnp.take(w_in, cols, axis=1).astype(BF16)
    w_wi = jnp.pad(w_in[:, 7 * GROUP_W + HEAD_DIM:], ((0, 0), (0, LANES - N_HEADS))).astype(BF16)
    inv_freq = 1.0 / (ROPE_THETA ** (np.arange(0, HEAD_DIM, 2, dtype=np.float32) / HEAD_DIM))
    invf = jnp.asarray(np.tile(inv_freq, 4)[None, :], F32)
    wcols = w_main.shape[1]
    const = lambda i: (0, 0)
    row = lambda i: (i, 0)
    return pl.pallas_call(
        _proj_kernel,
        grid=(n // tm,),
        in_specs=[pl.BlockSpec((tm, d), row), pl.BlockSpec((tm, 1), row),
                  pl.BlockSpec((1, d), const), pl.BlockSpec((1, LANES), const),
                  pl.BlockSpec((d, wcols), const, pipeline_mode=pl.Buffered(1)),
                  pl.BlockSpec((d, LANES), const)],
        out_specs=[pl.BlockSpec((tm, 3 * GROUP_W), row), pl.BlockSpec((tm, 3 * GROUP_W), row),
                   pl.BlockSpec((tm, GROUP_W + LANES), row), pl.BlockSpec((tm, LANES), row)],
        out_shape=[jax.ShapeDtypeStruct((n, 3 * GROUP_W), BF16),
                   jax.ShapeDtypeStruct((n, 3 * GROUP_W), BF16),
                   jax.ShapeDtypeStruct((n, GROUP_W + LANES), BF16),
                   jax.ShapeDtypeStruct((n, LANES), F32)],
        compiler_params=_cparams(1), name="in_proj_rope",
    )(x2, pos2, attn_norm.reshape(1, d), invf, w_main, w_wi)


def _head_lane_masks():
    lane = lax.broadcasted_iota(jnp.int32, (1, LANES), 1)
    first = (lane % HEAD_DIM) < HALF
    return first, jnp.logical_not(first)


def _dilated_kernel(q_ref, kc_ref, kp_ref, vc_ref, vp_ref, o_ref, lse_ref):
    nb = pl.program_id(2)
    n = WIN_STEPS
    a = lax.broadcasted_iota(jnp.int32, (n, 2 * n), 0)
    c = lax.broadcasted_iota(jnp.int32, (n, 2 * n), 1)
    band = (c >= a) & (c <= a + n) & ((nb > 0) | (c >= n))
    masks = _head_lane_masks()
    lane = lax.broadcasted_iota(jnp.int32, (1, LANES), 1)
    low = lane < HEAD_DIM
    for j in range(N_PAIRS):
        sl = slice(j * LANES, (j + 1) * LANES)
        q = q_ref[:, sl]
        kk = jnp.concatenate([kp_ref[:, sl], kc_ref[:, sl]], axis=0)
        vv = jnp.concatenate([vp_ref[:, sl], vc_ref[:, sl]], axis=0)
        outs, lses = [], []
        for hb in range(2):
            qm = jnp.where(masks[hb], q, jnp.zeros_like(q))
            s = _dot_t(qm, kk)
            s = jnp.where(band, s, -jnp.inf)
            m = jnp.max(s, axis=1, keepdims=True)
            p = jnp.exp(s - m)
            l = jnp.sum(p, axis=1, keepdims=True)
            outs.append(_dot(p.astype(BF16), vv) / l)
            lses.append(m + jnp.log(l))
        o_ref[:, sl] = jnp.where(low, outs[0], outs[1])
        lse_ref[:, sl] = jnp.where(low, lses[0], lses[1])


def _dilated_branch(za, b, t, dil):
    n = WIN_STEPS
    rows = t // dil
    zv = za.reshape(b, rows, dil * 3 * GROUP_W)
    blk = (None, n, GROUP_W)
    cur = lambda slot: (lambda bi, r, nb: (bi, nb, r * 3 + slot))
    prev = lambda slot: (lambda bi, r, nb: (bi, jnp.maximum(nb - 1, 0), r * 3 + slot))
    out_map = lambda bi, r, nb: (bi, nb, r)
    o, lse = pl.pallas_call(
        _dilated_kernel,
        grid=(b, dil, rows // n),
        in_specs=[pl.BlockSpec(blk, cur(0)), pl.BlockSpec(blk, cur(1)), pl.BlockSpec(blk, prev(1)),
                  pl.BlockSpec(blk, cur(2)), pl.BlockSpec(blk, prev(2))],
        out_specs=[pl.BlockSpec(blk, out_map), pl.BlockSpec(blk, out_map)],
        out_shape=[jax.ShapeDtypeStruct((b, rows, dil * GROUP_W), F32)] * 2,
        compiler_params=_cparams(3), name=f"dilated_attn_d{dil}",
    )(zv, zv, zv, zv, zv)
    return o.reshape(b * t, GROUP_W), lse.reshape(b * t, GROUP_W)


DSA_TQ = 256
P1_CHUNK = 256
P3_CHUNK = 256
CNT_ROWS = 32
BF16_ROWS = 16
BF16_MIN_NORMAL = 0x0080
CHECK_EVERY = 4


def _to_key(x):
    bits = pltpu.bitcast(x, jnp.int32)
    return bits ^ ((bits >> 31) & 0x7FFFFFFF)


def _dsa_kernel(qb_ref, qi_ref, wit_ref, kb_ref, vt_ref, ki_ref, ot_ref,
                key_ref, top_ref, qim_ref, qbm_ref, m_ref, l_ref, acc_ref, thr_ref, s_ref, *, topk):
    tq = DSA_TQ
    qblk = pl.program_id(1)
    t0 = qblk * tq
    n1 = qblk + 1
    masks = _head_lane_masks()

    for h in range(N_HEADS):
        j, hb = divmod(h, 2)
        sl = slice(j * LANES, (j + 1) * LANES)
        qi = qi_ref[:, sl]
        qb = qb_ref[:, sl]
        qim_ref[h] = jnp.where(masks[hb], qi, jnp.zeros_like(qi))
        qbm_ref[h] = jnp.where(masks[hb], qb, jnp.zeros_like(qb))

    def p1_chunk(c, diagonal):
        k0 = pl.multiple_of(c * P1_CHUNK, P1_CHUNK)
        kic = ki_ref[pl.ds(k0, P1_CHUNK), :]
        acc = jnp.zeros((P1_CHUNK, tq), F32)
        for h in range(N_HEADS):
            acc = acc + wit_ref[h:h + 1, :] * jnp.maximum(_dot_t(kic, qim_ref[h]), 0.0)
        acc = jnp.where(acc == 0.0, 0.0, acc)
        if diagonal:
            kpos = lax.broadcasted_iota(jnp.int32, (P1_CHUNK, tq), 0)
            qpos = lax.broadcasted_iota(jnp.int32, (P1_CHUNK, tq), 1)
            acc = jnp.where(kpos <= qpos, acc, -jnp.inf)
        key_ref[pl.ds(k0, P1_CHUNK), :] = _to_key(acc)
        top = pltpu.bitcast(pltpu.bitcast(acc, jnp.int32) & (-65536), F32)
        top_ref[pl.ds(k0, P1_CHUNK), :] = top.astype(BF16)

    def p1(c, carry):
        p1_chunk(c, False)
        return carry

    lax.fori_loop(0, qblk, p1, 0)
    p1_chunk(qblk, True)

    def count_ge(thr):
        def body(c, cnt):
            k0 = pl.multiple_of(c * P1_CHUNK, P1_CHUNK)
            x = jnp.where(key_ref[pl.ds(k0, P1_CHUNK), :] >= thr, 1, 0)
            return cnt + jnp.sum(x.reshape(P1_CHUNK // CNT_ROWS, CNT_ROWS, tq), axis=0)

        cnt = lax.fori_loop(0, n1, body, jnp.zeros((CNT_ROWS, tq), jnp.int32))
        return jnp.sum(cnt, axis=0, keepdims=True)

    def count_ge_top(mid16):
        pattern = mid16 ^ ((mid16 >> 15) & 0x7FFF)
        pattern = jnp.where((mid16 > 0) & (mid16 < BF16_MIN_NORMAL), BF16_MIN_NORMAL, pattern)
        thr_f = pltpu.bitcast(pattern << 16, F32)
        thr_b = jnp.broadcast_to(thr_f, (BF16_ROWS, tq)).astype(BF16)[None]
        one, zero = jnp.ones((), BF16), jnp.zeros((), BF16)

        def body(c, cnt):
            k0 = pl.multiple_of(c * P1_CHUNK, P1_CHUNK)
            x = top_ref[pl.ds(k0, P1_CHUNK), :].reshape(P1_CHUNK // BF16_ROWS, BF16_ROWS, tq)
            y = jnp.where(x >= thr_b, one, zero)
            parts = [y[r] for r in range(P1_CHUNK // BF16_ROWS)]
            while len(parts) > 1:
                parts = [a + b for a, b in zip(parts[::2], parts[1::2])]
            return cnt + parts[0].astype(F32)

        cnt = lax.fori_loop(0, n1, body, jnp.zeros((BF16_ROWS, tq), F32))
        return jnp.sum(cnt, axis=0, keepdims=True).astype(jnp.int32)

    def step(st, count_fn, final):
        lo, hi, thr, done, cnt_hi = st
        mid = (lo >> 1) + (hi >> 1) + (((lo & 1) + (hi & 1) + 1) >> 1)
        c = count_fn(mid)
        ge = c >= topk
        active = done == 0
        hit = active & (c == topk)
        new_lo = jnp.where(active & ge, mid, lo)
        new_hi = jnp.where(active & jnp.logical_not(ge), mid - 1, hi)
        new_cnt_hi = jnp.where(active & jnp.logical_not(ge), c, cnt_hi)
        new_thr = jnp.where(hit, mid, thr)
        new_done = jnp.where(hit, 1, done)
        if final:
            conv = active & jnp.logical_not(hit) & (new_lo == new_hi)
            new_thr = jnp.where(conv, new_lo, new_thr)
            new_done = jnp.where(conv, 2, new_done)
        return new_lo, new_hi, new_thr, new_done, new_cnt_hi

    t_row = t0 + lax.broadcasted_iota(jnp.int32, (1, tq), 1)
    all_sel = t_row + 1 <= topk
    done0 = all_sel.astype(jnp.int32)
    zeros = jnp.zeros((1, tq), jnp.int32)
    lo16 = jnp.full((1, tq), (KEY_NEG_INF + 1) >> 16, jnp.int32)
    hi16 = jnp.full((1, tq), 0x7F7F, jnp.int32)
    st = lax.fori_loop(0, 16, lambda _, s: step(s, count_ge_top, False),
                       (lo16, hi16, zeros, done0, zeros))
    lo16, _, thr16, done, cnt_hi = st
    lo = lo16 << 16
    thr = jnp.where(all_sel, KEY_NEG_INF + 1, thr16 << 16)

    def fine_cond(carry):
        it, st = carry
        return (it < 16 // CHECK_EVERY) & (jnp.min(st[3]) == 0)

    def fine_body(carry):
        it, st = carry
        return it + 1, lax.fori_loop(0, CHECK_EVERY, lambda _, s: step(s, count_ge, True), st)

    _, st = lax.while_loop(fine_cond, fine_body, (0, (lo, lo + 0xFFFF, thr, done, cnt_hi)))
    _, _, thr, done, cnt_hi = st
    thr_ref[...] = thr

    tie = done == 2

    @pl.when(jnp.max(done) == 2)
    def _():
        need = topk - cnt_hi
        sub_i = lax.broadcasted_iota(jnp.int32, (P1_CHUNK, tq), 0)

        def count_eq_le(jmax):
            def body(c, cnt):
                k0 = pl.multiple_of(c * P1_CHUNK, P1_CHUNK)
                kc = key_ref[pl.ds(k0, P1_CHUNK), :]
                x = jnp.where((kc == thr) & (sub_i + k0 <= jmax), 1, 0)
                return cnt + jnp.sum(x.reshape(P1_CHUNK // CNT_ROWS, CNT_ROWS, tq), axis=0)

            cnt = lax.fori_loop(0, n1, body, jnp.zeros((CNT_ROWS, tq), jnp.int32))
            return jnp.sum(cnt, axis=0, keepdims=True)

        def jbody(_, st):
            jlo, jhi = st
            jmid = (jlo + jhi) >> 1
            ok = count_eq_le(jmid) >= need
            return jnp.where(ok, jlo, jmid + 1), jnp.where(ok, jmid, jhi)

        jlo, _ = lax.fori_loop(0, 14, jbody, (jnp.zeros((1, tq), jnp.int32),
                                               jnp.broadcast_to(t0 + tq - 1, (1, tq)).astype(jnp.int32)))
        jcut = jnp.where(tie, jlo, 0x7FFFFFFF)

        def demote(c, carry):
            k0 = pl.multiple_of(c * P1_CHUNK, P1_CHUNK)
            kc = key_ref[pl.ds(k0, P1_CHUNK), :]
            key_ref[pl.ds(k0, P1_CHUNK), :] = jnp.where((kc == thr) & (sub_i + k0 > jcut), kc - 1, kc)
            return carry

        lax.fori_loop(0, n1, demote, 0)

    m_ref[...] = jnp.full(m_ref.shape, MASK_VALUE, F32)
    l_ref[...] = jnp.zeros(l_ref.shape, F32)
    acc_ref[...] = jnp.zeros(acc_ref.shape, F32)

    def p3(c, carry):
        k0 = pl.multiple_of(c * P3_CHUNK, P3_CHUNK)
        sel = key_ref[pl.ds(k0, P3_CHUNK), :] >= thr_ref[...]
        chunk_max = []
        for h in range(N_HEADS):
            j = h // 2
            kc = kb_ref[pl.ds(k0, P3_CHUNK), j * LANES:(j + 1) * LANES]
            s = jnp.where(sel, _dot_t(kc, qbm_ref[h]), MASK_VALUE)
            s_ref[h] = s
            chunk_max.append(jnp.max(s, axis=0, keepdims=True))
        for h in range(N_HEADS):
            rows = slice(h * HEAD_DIM, (h + 1) * HEAD_DIM)
            m_old = m_ref[h]
            m_new = jnp.maximum(m_old, chunk_max[h])
            alpha = jnp.exp2(m_old - m_new)
            p = jnp.exp2(s_ref[h] - m_new)
            l_ref[h] = alpha * l_ref[h] + jnp.sum(p, axis=0, keepdims=True)
            m_ref[h] = m_new
            pv = _dot(vt_ref[rows, pl.ds(k0, P3_CHUNK)], p.astype(BF16))
            acc_ref[rows, :] = alpha * acc_ref[rows, :] + pv
        return carry

    lax.fori_loop(0, n1 * (P1_CHUNK // P3_CHUNK), p3, 0)
    for h in range(N_HEADS):
        rows = slice(h * HEAD_DIM, (h + 1) * HEAD_DIM)
        ot_ref[rows, :] = (acc_ref[rows, :] / l_ref[h]).astype(BF16)


def _sparse_attention(zb, zi, wi, b, t, topk):
    tq = DSA_TQ
    zb3 = zb.reshape(b, t, 3 * GROUP_W)
    zi3 = zi.reshape(b, t, GROUP_W + LANES)
    vt = zb3[:, :, 2 * GROUP_W:].transpose(0, 2, 1)
    wit = wi.reshape(b, t, LANES)[:, :, :N_HEADS].transpose(0, 2, 1)
    qmap = lambda bi, qi: (bi, qi, 0)
    tmap = lambda bi, qi: (bi, 0, qi)
    out_t = pl.pallas_call(
        functools.partial(_dsa_kernel, topk=topk),
        grid=(b, t // tq),
        in_specs=[pl.BlockSpec((None, tq, GROUP_W), qmap),
                  pl.BlockSpec((None, tq, GROUP_W), qmap),
                  pl.BlockSpec((None, N_HEADS, tq), tmap),
                  pl.BlockSpec((None, t, GROUP_W), lambda bi, qi: (bi, 0, 1),
                               pipeline_mode=pl.Buffered(1)),
                  pl.BlockSpec((None, GROUP_W, t), lambda bi, qi: (bi, 0, 0),
                               pipeline_mode=pl.Buffered(1)),
                  pl.BlockSpec((None, t, LANES), lambda bi, qi: (bi, 0, GROUP_W // LANES),
                               pipeline_mode=pl.Buffered(1))],
        out_specs=pl.BlockSpec((None, GROUP_W, tq), tmap),
        out_shape=jax.ShapeDtypeStruct((b, GROUP_W, t), BF16),
        scratch_shapes=[pltpu.VMEM((t, tq), jnp.int32),
                        pltpu.VMEM((t, tq), BF16),
                        pltpu.VMEM((N_HEADS, tq, LANES), BF16),
                        pltpu.VMEM((N_HEADS, tq, LANES), BF16),
                        pltpu.VMEM((N_HEADS, 1, tq), F32),
                        pltpu.VMEM((N_HEADS, 1, tq), F32),
                        pltpu.VMEM((GROUP_W, tq), F32),
                        pltpu.VMEM((1, tq), jnp.int32),
                        pltpu.VMEM((N_HEADS, P3_CHUNK, tq), F32)],
        compiler_params=_cparams(2), name="dsa_attn",
    )(zb3, zi3, wit, zb3, vt, zi3)
    return out_t.transpose(0, 2, 1).reshape(b * t, GROUP_W)


def _out_proj_kernel(x_ref, o1_ref, o2_ref, o3_ref, l1_ref, l2_ref, l3_ref, ob_ref, wo_ref, y_ref):
    l1, l2, l3 = l1_ref[...], l2_ref[...], l3_ref[...]
    m = jnp.maximum(jnp.maximum(l1, l2), l3)
    e1, e2, e3 = jnp.exp(l1 - m), jnp.exp(l2 - m), jnp.exp(l3 - m)
    out_a = (e1 * o1_ref[...] + e2 * o2_ref[...] + e3 * o3_ref[...]) / (e1 + e2 + e3)
    y = x_ref[...] + _dot(out_a.astype(BF16), wo_ref[0:GROUP_W, :])
    y_ref[...] = y + _dot(ob_ref[...], wo_ref[GROUP_W:2 * GROUP_W, :])


def _out_proj(x2, branches, ob, w_o, tm):
    n, d = x2.shape
    row = lambda i: (i, 0)
    gspec = pl.BlockSpec((tm, GROUP_W), row)
    (o1, l1), (o2, l2), (o3, l3) = branches
    return pl.pallas_call(
        _out_proj_kernel,
        grid=(n // tm,),
        in_specs=[pl.BlockSpec((tm, d), row)] + [gspec] * 7 + [pl.BlockSpec((2 * GROUP_W, d), lambda i: (0, 0))],
        out_specs=pl.BlockSpec((tm, d), row),
        out_shape=jax.ShapeDtypeStruct((n, d), F32),
        compiler_params=_cparams(1), name="merge_out_proj",
    )(x2, o1, o2, o3, l1, l2, l3, ob, w_o.astype(BF16))


HALO = 8


def _ffn_kernel(x_ref, g_ref, wup_ref, cw_ref, cb_ref, wdn_ref, y_ref,
                acc_ref, buf_ref, tail_ref, *, tiles_per_seq):
    tm = x_ref.shape[0]
    i = pl.program_id(0)

    @pl.when(i % tiles_per_seq == 0)
    def _():
        tail_ref[...] = jnp.zeros(tail_ref.shape, F32)

    x = x_ref[...]
    h = _rms(x, g_ref[...]).astype(BF16)
    acc_ref[...] = x

    def conv(c):
        u = _dot(h, wup_ref[c])
        buf_ref[0:HALO, :] = tail_ref[c]
        buf_ref[HALO:HALO + tm, :] = u
        tail_ref[c] = u[tm - HALO:, :]
        w = cw_ref[c]
        y = cb_ref[c] + w[2:3, :] * u
        y = y + w[1:2, :] * buf_ref[HALO - 1:HALO - 1 + tm, :]
        return y + w[0:1, :] * buf_ref[HALO - 2:HALO - 2 + tm, :]

    def body(c, carry):
        gate = conv(c)
        up = conv(c + N_FF_CHUNKS)
        act = (gate * jax.nn.sigmoid(gate) * up).astype(BF16)
        acc_ref[...] += _dot(act, wdn_ref[c])
        return carry

    lax.fori_loop(0, N_FF_CHUNKS, body, 0, unroll=True)
    y_ref[...] = acc_ref[...]


def _ffn(x2, t, ffn_norm, w_up, conv_w, conv_b, w_down, tm):
    n, d = x2.shape
    nc2 = 2 * N_FF_CHUNKS
    wup = w_up.reshape(d, nc2, FF_CHUNK).transpose(1, 0, 2).astype(BF16)
    cw = conv_w.reshape(CONV_WIDTH, nc2, FF_CHUNK).transpose(1, 0, 2)
    cb = conv_b.reshape(nc2, 1, FF_CHUNK)
    wdn = w_down.reshape(N_FF_CHUNKS, FF_CHUNK, d).astype(BF16)
    row = lambda i: (i, 0)
    c3 = lambda i: (0, 0, 0)
    return pl.pallas_call(
        functools.partial(_ffn_kernel, tiles_per_seq=t // tm),
        grid=(n // tm,),
        in_specs=[pl.BlockSpec((tm, d), row), pl.BlockSpec((1, d), lambda i: (0, 0)),
                  pl.BlockSpec((nc2, d, FF_CHUNK), c3, pipeline_mode=pl.Buffered(1)),
                  pl.BlockSpec((nc2, CONV_WIDTH, FF_CHUNK), c3),
                  pl.BlockSpec((nc2, 1, FF_CHUNK), c3),
                  pl.BlockSpec((N_FF_CHUNKS, FF_CHUNK, d), c3, pipeline_mode=pl.Buffered(1))],
        out_specs=pl.BlockSpec((tm, d), row),
        out_shape=jax.ShapeDtypeStruct((n, d), F32),
        scratch_shapes=[pltpu.VMEM((tm, d), F32), pltpu.VMEM((tm + HALO, FF_CHUNK), F32),
                        pltpu.VMEM((nc2, HALO, FF_CHUNK), F32)],
        compiler_params=_cparams(1), name="conv_ffn",
    )(x2, ffn_norm.reshape(1, d), wup, cw, cb, wdn)


def _ple_kernel(x_ref, p_ref, g_ref, wg_ref, wp_ref, gf_ref, y_ref, *, final):
    x = x_ref[...]
    gate = jax.nn.sigmoid(_dot(_rms(x, g_ref[...]).astype(BF16), wg_ref[...]))
    x = x + gate * _dot(p_ref[...].astype(BF16), wp_ref[...])
    y_ref[...] = _rms(x, gf_ref[...]) if final else x


def _ple_final(x2, p2, ple_norm, w_gate, w_proj, final_norm, final, tm):
    n, d = x2.shape
    pd = p2.shape[1]
    row = lambda i: (i, 0)
    const = lambda i: (0, 0)
    return pl.pallas_call(
        functools.partial(_ple_kernel, final=final),
        grid=(n // tm,),
        in_specs=[pl.BlockSpec((tm, d), row), pl.BlockSpec((tm, pd), row), pl.BlockSpec((1, d), const),
                  pl.BlockSpec((d, d), const), pl.BlockSpec((pd, d), const), pl.BlockSpec((1, d), const)],
        out_specs=pl.BlockSpec((tm, d), row),
        out_shape=jax.ShapeDtypeStruct((n, d), F32),
        compiler_params=_cparams(1), name="ple_final_norm",
    )(x2, p2, ple_norm.reshape(1, d), w_gate.astype(BF16), w_proj.astype(BF16), final_norm.reshape(1, d))


def kernel(x, p, positions, attn_norm, w_in, w_o, ffn_norm, w_up, conv_w, conv_b, w_down,
           ple_norm, w_ple_gate, w_ple_proj, final_norm):
    b, t, d = x.shape
    depth = p.shape[0]
    n = b * t
    topk = min(TOPK_MAX, t // 4)
    assert t % DILATED_PAIRS[-1][0] == 0 and t % DSA_TQ == 0
    tm = 256
    x2 = x.reshape(n, d)
    pos2 = positions.reshape(n, 1)
    for i in range(depth):
        za, zb, zi, wi = _project(x2, pos2, attn_norm[i], w_in[i], 2 * tm)
        branches = [_dilated_branch(za, b, t, dil) for _, dil in DILATED_PAIRS]
        ob = _sparse_attention(zb, zi, wi, b, t, topk)
        x2 = _out_proj(x2, branches, ob, w_o[i], tm)
        x2 = _ffn(x2, t, ffn_norm[i], w_up[i], conv_w[i], conv_b[i], w_down[i], 2 * tm)
        x2 = _ple_final(x2, p[i].reshape(n, -1), ple_norm[i], w_ple_gate[i], w_ple_proj[i],
                        final_norm, i == depth - 1, tm)
    return x2.reshape(b, t, d)
```

```python
import functools
import math

import numpy as np
import jax
import jax.numpy as jnp
from jax import lax
from jax.experimental import pallas as pl
from jax.experimental.pallas import tpu as pltpu

F32 = jnp.float32
BF16 = jnp.bfloat16

HEAD_DIM = 64
HALF = HEAD_DIM // 2
N_HEADS = 8
GROUP_W = N_HEADS * HEAD_DIM
N_PAIRS = N_HEADS // 2
LANES = 128
DILATED_PAIRS = ((128, 1), (512, 4), (2048, 16))
WIN_STEPS = 128
TOPK_MAX = 256
D_FF = 2816
FF_CHUNK = 256
N_FF_CHUNKS = D_FF // FF_CHUNK
CONV_WIDTH = 3
ROPE_THETA = 10000.0
RMS_EPS = 1e-6
MASK_VALUE = -1e30
KEY_NEG_INF = -2139095041
VMEM_LIMIT = 56 * 1024 * 1024


def _cparams(n_axes):
    return pltpu.CompilerParams(dimension_semantics=("arbitrary",) * n_axes,
                                vmem_limit_bytes=VMEM_LIMIT)


def _dot_t(a, b):
    return lax.dot_general(a, b, (((1,), (1,)), ((), ())), preferred_element_type=F32)


def _dot(a, b):
    return jnp.dot(a, b, preferred_element_type=F32)


def _rms(x, g):
    ms = jnp.mean(x * x, axis=-1, keepdims=True)
    return (x * lax.rsqrt(ms + RMS_EPS)) * g


def _pair_perm():
    idx = []
    for j in range(N_PAIRS):
        a, b = 2 * j * HEAD_DIM, (2 * j + 1) * HEAD_DIM
        idx += list(range(a, a + HALF)) + list(range(b, b + HALF))
        idx += list(range(a + HALF, a + HEAD_DIM)) + list(range(b + HALF, b + HEAD_DIM))
    return np.asarray(idx, np.int32)


def _in_proj_columns():
    perm = _pair_perm()
    nat = np.arange(GROUP_W, dtype=np.int32)
    cols = []
    for g, roped in enumerate((True, True, False, True, True, False, True)):
        cols.append(g * GROUP_W + (perm if roped else nat))
    kbase = 7 * GROUP_W
    k1 = kbase + np.arange(HALF, dtype=np.int32)
    k2 = kbase + HALF + np.arange(HALF, dtype=np.int32)
    cols.append(np.concatenate([k1, k1, k2, k2]))
    return np.concatenate(cols)


def _proj_kernel(x_ref, pos_ref, g_ref, invf_ref, w_ref, wiw_ref, za_ref, zb_ref, zi_ref, wi_ref):
    x = x_ref[...]
    h32 = _rms(x, g_ref[...])
    h = h32.astype(BF16)
    ang = pos_ref[...].astype(F32) * invf_ref[...]
    lane = lax.broadcasted_iota(jnp.int32, (1, LANES), 1)
    cos = jnp.cos(ang)
    sin = jnp.sin(ang) * jnp.where(lane < 2 * HALF, -1.0, 1.0)

    def rope(z):
        return z * cos + pltpu.roll(z, 2 * HALF, 1) * sin

    q_scale = HEAD_DIM ** -0.5
    plan = ((za_ref, 0, True, q_scale), (za_ref, 1, True, 1.0), (za_ref, 2, False, 1.0),
            (zb_ref, 0, True, q_scale * math.log2(math.e)), (zb_ref, 1, True, 1.0), (zb_ref, 2, False, 1.0),
            (zi_ref, 0, True, q_scale))
    for g, (out_ref, slot, roped, scale) in enumerate(plan):
        z = _dot(h, w_ref[:, g * GROUP_W:(g + 1) * GROUP_W])
        for j in range(N_PAIRS):
            zj = z[:, j * LANES:(j + 1) * LANES]
            if roped:
                zj = rope(zj)
            if scale != 1.0:
                zj = zj * scale
            out_ref[:, slot * GROUP_W + j * LANES: slot * GROUP_W + (j + 1) * LANES] = zj.astype(BF16)
    zk = _dot(h, w_ref[:, 7 * GROUP_W: 7 * GROUP_W + LANES])
    zi_ref[:, GROUP_W:GROUP_W + LANES] = rope(zk).astype(BF16)
    wi_ref[...] = _dot(h, wiw_ref[...]) * (N_HEADS ** -0.5)


def _project(x2, pos2, attn_norm, w_in, tm):
    n, d = x2.shape
    cols = _in_proj_columns()
    w_main = jnp.take(w_in, cols, axis=1).astype(BF16)
    w_wi = jnp.pad(w_in[:, 7 * GROUP_W + HEAD_DIM:], ((0, 0), (0, LANES - N_HEADS))).astype(BF16)
    inv_freq = 1.0 / (ROPE_THETA ** (np.arange(0, HEAD_DIM, 2, dtype=np.float32) / HEAD_DIM))
    invf = jnp.asarray(np.tile(inv_freq, 4)[None, :], F32)
    wcols = w_main.shape[1]
    const = lambda i: (0, 0)
    row = lambda i: (i, 0)
    return pl.pallas_call(
        _proj_kernel,
        grid=(n // tm,),
        in_specs=[pl.BlockSpec((tm, d), row), pl.BlockSpec((tm, 1), row),
                  pl.BlockSpec((1, d), const), pl.BlockSpec((1, LANES), const),
                  pl.BlockSpec((d, wcols), const, pipeline_mode=pl.Buffered(1)),
                  pl.BlockSpec((d, LANES), const)],
        out_specs=[pl.BlockSpec((tm, 3 * GROUP_W), row), pl.BlockSpec((tm, 3 * GROUP_W), row),
                   pl.BlockSpec((tm, GROUP_W + LANES), row), pl.BlockSpec((tm, LANES), row)],
        out_shape=[jax.ShapeDtypeStruct((n, 3 * GROUP_W), BF16),
                   jax.ShapeDtypeStruct((n, 3 * GROUP_W), BF16),
                   jax.ShapeDtypeStruct((n, GROUP_W + LANES), BF16),
                   jax.ShapeDtypeStruct((n, LANES), F32)],
        compiler_params=_cparams(1), name="in_proj_rope",
    )(x2, pos2, attn_norm.reshape(1, d), invf, w_main, w_wi)


def _head_lane_masks():
    lane = lax.broadcasted_iota(jnp.int32, (1, LANES), 1)
    first = (lane % HEAD_DIM) < HALF
    return first, jnp.logical_not(first)


def _dilated_kernel(q_ref, kc_ref, kp_ref, vc_ref, vp_ref, o_ref, lse_ref):
    nb = pl.program_id(2)
    n = WIN_STEPS
    a = lax.broadcasted_iota(jnp.int32, (n, 2 * n), 0)
    c = lax.broadcasted_iota(jnp.int32, (n, 2 * n), 1)
    band = (c >= a) & (c <= a + n) & ((nb > 0) | (c >= n))
    masks = _head_lane_masks()
    lane = lax.broadcasted_iota(jnp.int32, (1, LANES), 1)
    low = lane < HEAD_DIM
    for j in range(N_PAIRS):
        sl = slice(j * LANES, (j + 1) * LANES)
        q = q_ref[:, sl]
        kk = jnp.concatenate([kp_ref[:, sl], kc_ref[:, sl]], axis=0)
        vv = jnp.concatenate([vp_ref[:, sl], vc_ref[:, sl]], axis=0)
        outs, lses = [], []
        for hb in range(2):
            qm = jnp.where(masks[hb], q, jnp.zeros_like(q))
            s = _dot_t(qm, kk)
            s = jnp.where(band, s, -jnp.inf)
            m = jnp.max(s, axis=1, keepdims=True)
            p = jnp.exp(s - m)
            l = jnp.sum(p, axis=1, keepdims=True)
            outs.append(_dot(p.astype(BF16), vv) / l)
            lses.append(m + jnp.log(l))
        o_ref[:, sl] = jnp.where(low, outs[0], outs[1])
        lse_ref[:, sl] = jnp.where(low, lses[0], lses[1])


def _dilated_branch(za, b, t, dil):
    n = WIN_STEPS
    rows = t // dil
    zv = za.reshape(b, rows, dil * 3 * GROUP_W)
    blk = (None, n, GROUP_W)
    cur = lambda slot: (lambda bi, r, nb: (bi, nb, r * 3 + slot))
    prev = lambda slot: (lambda bi, r, nb: (bi, jnp.maximum(nb - 1, 0), r * 3 + slot))
    out_map = lambda bi, r, nb: (bi, nb, r)
    o, lse = pl.pallas_call(
        _dilated_kernel,
        grid=(b, dil, rows // n),
        in_specs=[pl.BlockSpec(blk, cur(0)), pl.BlockSpec(blk, cur(1)), pl.BlockSpec(blk, prev(1)),
                  pl.BlockSpec(blk, cur(2)), pl.BlockSpec(blk, prev(2))],
        out_specs=[pl.BlockSpec(blk, out_map), pl.BlockSpec(blk, out_map)],
        out_shape=[jax.ShapeDtypeStruct((b, rows, dil * GROUP_W), F32)] * 2,
        compiler_params=_cparams(3), name=f"dilated_attn_d{dil}",
    )(zv, zv, zv, zv, zv)
    return o.reshape(b * t, GROUP_W), lse.reshape(b * t, GROUP_W)


DSA_TQ = 256
P1_CHUNK = 256
P3_CHUNK = 256
CNT_ROWS = 32
BF16_ROWS = 16
VT_ROWS = HEAD_DIM + BF16_ROWS
BF16_MIN_NORMAL = 0x0080
CHECK_EVERY = 4


def _to_key(x):
    bits = pltpu.bitcast(x, jnp.int32)
    return bits ^ ((bits >> 31) & 0x7FFFFFFF)


def _dsa_kernel(qb_ref, qi_ref, wit_ref, kb_ref, vt_ref, ki_ref, ot_ref,
                key_ref, top_ref, qim_ref, qbm_ref, m_ref, acc_ref, thr_ref, s_ref, mx_ref,
                *, topk):
    assert P3_CHUNK == P1_CHUNK == DSA_TQ
    tq = DSA_TQ
    qblk = pl.program_id(1)
    t0 = qblk * tq
    n1 = qblk + 1
    masks = _head_lane_masks()

    for h in range(N_HEADS):
        j, hb = divmod(h, 2)
        sl = slice(j * LANES, (j + 1) * LANES)
        qi = qi_ref[:, sl]
        qb = qb_ref[:, sl]
        qim_ref[h] = jnp.where(masks[hb], qi, jnp.zeros_like(qi))
        qbm_ref[h] = jnp.where(masks[hb], qb, jnp.zeros_like(qb))

    def p1_chunk(c, diagonal):
        k0 = pl.multiple_of(c * P1_CHUNK, P1_CHUNK)
        kic = ki_ref[pl.ds(k0, P1_CHUNK), :]
        acc = jnp.zeros((P1_CHUNK, tq), F32)
        for h in range(N_HEADS):
            acc = acc + wit_ref[h:h + 1, :] * jnp.maximum(_dot_t(kic, qim_ref[h]), 0.0)
        acc = jnp.where(acc == 0.0, 0.0, acc)
        if diagonal:
            kpos = lax.broadcasted_iota(jnp.int32, (P1_CHUNK, tq), 0)
            qpos = lax.broadcasted_iota(jnp.int32, (P1_CHUNK, tq), 1)
            acc = jnp.where(kpos <= qpos, acc, -jnp.inf)
        key_ref[pl.ds(k0, P1_CHUNK), :] = _to_key(acc)
        top = pltpu.bitcast(pltpu.bitcast(acc, jnp.int32) & (-65536), F32)
        top_ref[pl.ds(k0, P1_CHUNK), :] = top.astype(BF16)

    def p1_pair(i, carry):
        p1_chunk(2 * i, False)
        p1_chunk(2 * i + 1, False)
        return carry

    lax.fori_loop(0, qblk // 2, p1_pair, 0)

    @pl.when(qblk % 2 == 1)
    def _():
        p1_chunk(qblk - 1, False)

    p1_chunk(qblk, True)

    def count_ge(thr):
        def body(c, cnt):
            k0 = pl.multiple_of(c * P1_CHUNK, P1_CHUNK)
            x = jnp.where(key_ref[pl.ds(k0, P1_CHUNK), :] >= thr, 1, 0)
            return cnt + jnp.sum(x.reshape(P1_CHUNK // CNT_ROWS, CNT_ROWS, tq), axis=0)

        cnt = lax.fori_loop(0, n1, body, jnp.zeros((CNT_ROWS, tq), jnp.int32))
        return jnp.sum(cnt, axis=0, keepdims=True)

    def count_ge_top(mid16):
        pattern = mid16 ^ ((mid16 >> 15) & 0x7FFF)
        pattern = jnp.where((mid16 > 0) & (mid16 < BF16_MIN_NORMAL), BF16_MIN_NORMAL, pattern)
        thr_f = pltpu.bitcast(pattern << 16, F32)
        thr_b = jnp.broadcast_to(thr_f, (BF16_ROWS, tq)).astype(BF16)[None]
        one, zero = jnp.ones((), BF16), jnp.zeros((), BF16)

        def body(c, cnt):
            k0 = pl.multiple_of(c * P1_CHUNK, P1_CHUNK)
            x = top_ref[pl.ds(k0, P1_CHUNK), :].reshape(P1_CHUNK // BF16_ROWS, BF16_ROWS, tq)
            y = jnp.where(x >= thr_b, one, zero)
            parts = [y[r] for r in range(P1_CHUNK // BF16_ROWS)]
            while len(parts) > 1:
                parts = [a + b for a, b in zip(parts[::2], parts[1::2])]
            return cnt + parts[0].astype(F32)

        cnt = lax.fori_loop(0, n1, body, jnp.zeros((BF16_ROWS, tq), F32))
        return jnp.sum(cnt, axis=0, keepdims=True).astype(jnp.int32)

    def step(st, count_fn, final):
        lo, hi, thr, done, cnt_hi = st
        mid = (lo >> 1) + (hi >> 1) + (((lo & 1) + (hi & 1) + 1) >> 1)
        c = count_fn(mid)
        ge = c >= topk
        active = done == 0
        hit = active & (c == topk)
        new_lo = jnp.where(active & ge, mid, lo)
        new_hi = jnp.where(active & jnp.logical_not(ge), mid - 1, hi)
        new_cnt_hi = jnp.where(active & jnp.logical_not(ge), c, cnt_hi)
        new_thr = jnp.where(hit, mid, thr)
        new_done = jnp.where(hit, 1, done)
        if final:
            conv = active & jnp.logical_not(hit) & (new_lo == new_hi)
            new_thr = jnp.where(conv, new_lo, new_thr)
            new_done = jnp.where(conv, 2, new_done)
        return new_lo, new_hi, new_thr, new_done, new_cnt_hi

    t_row = t0 + lax.broadcasted_iota(jnp.int32, (1, tq), 1)
    all_sel = t_row + 1 <= topk
    done0 = all_sel.astype(jnp.int32)
    zeros = jnp.zeros((1, tq), jnp.int32)
    lo16 = jnp.full((1, tq), (KEY_NEG_INF + 1) >> 16, jnp.int32)
    hi16 = jnp.full((1, tq), 0x7F7F, jnp.int32)
    st = lax.fori_loop(0, 16, lambda _, s: step(s, count_ge_top, False),
                       (lo16, hi16, zeros, done0, zeros))
    lo16, _, thr16, done, cnt_hi = st
    lo = lo16 << 16
    thr = jnp.where(all_sel, KEY_NEG_INF + 1, thr16 << 16)

    def fine_cond(carry):
        it, st = carry
        return (it < 16 // CHECK_EVERY) & (jnp.min(st[3]) == 0)

    def fine_body(carry):
        it, st = carry
        return it + 1, lax.fori_loop(0, CHECK_EVERY, lambda _, s: step(s, count_ge, True), st)

    _, st = lax.while_loop(fine_cond, fine_body, (0, (lo, lo + 0xFFFF, thr, done, cnt_hi)))
    _, _, thr, done, cnt_hi = st
    thr_ref[...] = thr

    tie = done == 2

    @pl.when(jnp.max(done) == 2)
    def _():
        need = topk - cnt_hi
        sub_i = lax.broadcasted_iota(jnp.int32, (P1_CHUNK, tq), 0)

        def count_eq_le(jmax):
            def body(c, cnt):
                k0 = pl.multiple_of(c * P1_CHUNK, P1_CHUNK)
                kc = key_ref[pl.ds(k0, P1_CHUNK), :]
                x = jnp.where((kc == thr) & (sub_i + k0 <= jmax), 1, 0)
                return cnt + jnp.sum(x.reshape(P1_CHUNK // CNT_ROWS, CNT_ROWS, tq), axis=0)

            cnt = lax.fori_loop(0, n1, body, jnp.zeros((CNT_ROWS, tq), jnp.int32))
            return jnp.sum(cnt, axis=0, keepdims=True)

        def jbody(_, st):
            jlo, jhi = st
            jmid = (jlo + jhi) >> 1
            ok = count_eq_le(jmid) >= need
            return jnp.where(ok, jlo, jmid + 1), jnp.where(ok, jmid, jhi)

        jlo, _ = lax.fori_loop(0, 14, jbody, (jnp.zeros((1, tq), jnp.int32),
                                               jnp.broadcast_to(t0 + tq - 1, (1, tq)).astype(jnp.int32)))
        jcut = jnp.where(tie, jlo, 0x7FFFFFFF)

        def demote(c, carry):
            k0 = pl.multiple_of(c * P1_CHUNK, P1_CHUNK)
            kc = key_ref[pl.ds(k0, P1_CHUNK), :]
            key_ref[pl.ds(k0, P1_CHUNK), :] = jnp.where((kc == thr) & (sub_i + k0 > jcut), kc - 1, kc)
            return carry

        lax.fori_loop(0, n1, demote, 0)

    m_ref[...] = jnp.full(m_ref.shape, MASK_VALUE, F32)
    acc_ref[...] = jnp.zeros(acc_ref.shape, F32)

    def scores(c, slot):
        k0 = pl.multiple_of(c * P3_CHUNK, P3_CHUNK)
        sel = key_ref[pl.ds(k0, P3_CHUNK), :] >= thr_ref[...]
        for h in range(N_HEADS):
            j = h // 2
            kc = kb_ref[pl.ds(k0, P3_CHUNK), j * LANES:(j + 1) * LANES]
            s = jnp.where(sel, _dot_t(kc, qbm_ref[h]), MASK_VALUE)
            s_ref[slot, h] = s
            mx_ref[slot, h] = jnp.max(s, axis=0, keepdims=True)

    def accumulate(c, slot):
        k0 = pl.multiple_of(c * P3_CHUNK, P3_CHUNK)
        for h in range(N_HEADS):
            rows = slice(h * VT_ROWS, (h + 1) * VT_ROWS)
            m_old = m_ref[h]
            m_new = jnp.maximum(m_old, mx_ref[slot, h])
            alpha = jnp.exp2(m_old - m_new)
            p = jnp.exp2((s_ref[slot, h] - m_new).astype(BF16))
            m_ref[h] = m_new
            pv = _dot(vt_ref[rows, pl.ds(k0, P3_CHUNK)], p)
            acc_ref[rows, :] = alpha * acc_ref[rows, :] + pv

    scores(0, 0)

    def p3_pair(i, carry):
        c = 2 * i
        scores(c + 1, 1)
        accumulate(c, 0)
        scores(c + 2, 0)
        accumulate(c + 1, 1)
        return carry

    lax.fori_loop(0, (n1 - 1) // 2, p3_pair, 0)

    @pl.when(n1 % 2 == 1)
    def _():
        accumulate(n1 - 1, 0)

    @pl.when(n1 % 2 == 0)
    def _():
        scores(n1 - 1, 1)
        accumulate(n1 - 2, 0)
        accumulate(n1 - 1, 1)

    for h in range(N_HEADS):
        base = h * VT_ROWS
        denom = acc_ref[base + HEAD_DIM:base + HEAD_DIM + 1, :]
        ot_ref[h * HEAD_DIM:(h + 1) * HEAD_DIM, :] = (acc_ref[base:base + HEAD_DIM, :] / denom).astype(BF16)


def _sparse_attention(zb, zi, wi, b, t, topk):
    tq = DSA_TQ
    zb3 = zb.reshape(b, t, 3 * GROUP_W)
    zi3 = zi.reshape(b, t, GROUP_W + LANES)
    vt = zb3[:, :, 2 * GROUP_W:].transpose(0, 2, 1).reshape(b, N_HEADS, HEAD_DIM, t)
    pad = jnp.zeros((b, N_HEADS, VT_ROWS - HEAD_DIM, t), BF16).at[:, :, 0, :].set(1.0)
    vt = jnp.concatenate([vt, pad], axis=2).reshape(b, N_HEADS * VT_ROWS, t)
    wit = wi.reshape(b, t, LANES)[:, :, :N_HEADS].transpose(0, 2, 1)
    qmap = lambda bi, qi: (bi, qi, 0)
    tmap = lambda bi, qi: (bi, 0, qi)
    out_t = pl.pallas_call(
        functools.partial(_dsa_kernel, topk=topk),
        grid=(b, t // tq),
        in_specs=[pl.BlockSpec((None, tq, GROUP_W), qmap),
                  pl.BlockSpec((None, tq, GROUP_W), qmap),
                  pl.BlockSpec((None, N_HEADS, tq), tmap),
                  pl.BlockSpec((None, t, GROUP_W), lambda bi, qi: (bi, 0, 1),
                               pipeline_mode=pl.Buffered(1)),
                  pl.BlockSpec((None, N_HEADS * VT_ROWS, t), lambda bi, qi: (bi, 0, 0),
                               pipeline_mode=pl.Buffered(1)),
                  pl.BlockSpec((None, t, LANES), lambda bi, qi: (bi, 0, GROUP_W // LANES),
                               pipeline_mode=pl.Buffered(1))],
        out_specs=pl.BlockSpec((None, GROUP_W, tq), tmap),
        out_shape=jax.ShapeDtypeStruct((b, GROUP_W, t), BF16),
        scratch_shapes=[pltpu.VMEM((t, tq), jnp.int32),
                        pltpu.VMEM((t, tq), BF16),
                        pltpu.VMEM((N_HEADS, tq, LANES), BF16),
                        pltpu.VMEM((N_HEADS, tq, LANES), BF16),
                        pltpu.VMEM((N_HEADS, 1, tq), F32),
                        pltpu.VMEM((N_HEADS * VT_ROWS, tq), F32),
                        pltpu.VMEM((1, tq), jnp.int32),
                        pltpu.VMEM((2, N_HEADS, P3_CHUNK, tq), F32),
                        pltpu.VMEM((2, N_HEADS, 1, tq), F32)],
        compiler_params=_cparams(2), name="dsa_attn",
    )(zb3, zi3, wit, zb3, vt, zi3)
    return out_t.transpose(0, 2, 1).reshape(b * t, GROUP_W)


def _out_proj_kernel(x_ref, o1_ref, o2_ref, o3_ref, l1_ref, l2_ref, l3_ref, ob_ref, wo_ref, y_ref):
    l1, l2, l3 = l1_ref[...], l2_ref[...], l3_ref[...]
    m = jnp.maximum(jnp.maximum(l1, l2), l3)
    e1, e2, e3 = jnp.exp(l1 - m), jnp.exp(l2 - m), jnp.exp(l3 - m)
    out_a = (e1 * o1_ref[...] + e2 * o2_ref[...] + e3 * o3_ref[...]) / (e1 + e2 + e3)
    y = x_ref[...] + _dot(out_a.astype(BF16), wo_ref[0:GROUP_W, :])
    y_ref[...] = y + _dot(ob_ref[...], wo_ref[GROUP_W:2 * GROUP_W, :])


def _out_proj(x2, branches, ob, w_o, tm):
    n, d = x2.shape
    row = lambda i: (i, 0)
    gspec = pl.BlockSpec((tm, GROUP_W), row)
    (o1, l1), (o2, l2), (o3, l3) = branches
    return pl.pallas_call(
        _out_proj_kernel,
        grid=(n // tm,),
        in_specs=[pl.BlockSpec((tm, d), row)] + [gspec] * 7 + [pl.BlockSpec((2 * GROUP_W, d), lambda i: (0, 0))],
        out_specs=pl.BlockSpec((tm, d), row),
        out_shape=jax.ShapeDtypeStruct((n, d), F32),
        compiler_params=_cparams(1), name="merge_out_proj",
    )(x2, o1, o2, o3, l1, l2, l3, ob, w_o.astype(BF16))


HALO = 8


def _ffn_kernel(x_ref, g_ref, wup_ref, cw_ref, cb_ref, wdn_ref, y_ref,
                acc_ref, buf_ref, tail_ref, *, tiles_per_seq):
    tm = x_ref.shape[0]
    i = pl.program_id(0)

    @pl.when(i % tiles_per_seq == 0)
    def _():
        tail_ref[...] = jnp.zeros(tail_ref.shape, F32)

    x = x_ref[...]
    h = _rms(x, g_ref[...]).astype(BF16)
    acc_ref[...] = x

    def conv(c):
        u = _dot(h, wup_ref[c])
        buf_ref[0:HALO, :] = tail_ref[c]
        buf_ref[HALO:HALO + tm, :] = u
        tail_ref[c] = u[tm - HALO:, :]
        w = cw_ref[c]
        y = cb_ref[c] + w[2:3, :] * u
        y = y + w[1:2, :] * buf_ref[HALO - 1:HALO - 1 + tm, :]
        return y + w[0:1, :] * buf_ref[HALO - 2:HALO - 2 + tm, :]

    def body(c, carry):
        gate = conv(c)
        up = conv(c + N_FF_CHUNKS)
        act = (gate * jax.nn.sigmoid(gate) * up).astype(BF16)
        acc_ref[...] += _dot(act, wdn_ref[c])
        return carry

    lax.fori_loop(0, N_FF_CHUNKS, body, 0, unroll=True)
    y_ref[...] = acc_ref[...]


def _ffn(x2, t, ffn_norm, w_up, conv_w, conv_b, w_down, tm):
    n, d = x2.shape
    nc2 = 2 * N_FF_CHUNKS
    wup = w_up.reshape(d, nc2, FF_CHUNK).transpose(1, 0, 2).astype(BF16)
    cw = conv_w.reshape(CONV_WIDTH, nc2, FF_CHUNK).transpose(1, 0, 2)
    cb = conv_b.reshape(nc2, 1, FF_CHUNK)
    wdn = w_down.reshape(N_FF_CHUNKS, FF_CHUNK, d).astype(BF16)
    row = lambda i: (i, 0)
    c3 = lambda i: (0, 0, 0)
    return pl.pallas_call(
        functools.partial(_ffn_kernel, tiles_per_seq=t // tm),
        grid=(n // tm,),
        in_specs=[pl.BlockSpec((tm, d), row), pl.BlockSpec((1, d), lambda i: (0, 0)),
                  pl.BlockSpec((nc2, d, FF_CHUNK), c3, pipeline_mode=pl.Buffered(1)),
                  pl.BlockSpec((nc2, CONV_WIDTH, FF_CHUNK), c3),
                  pl.BlockSpec((nc2, 1, FF_CHUNK), c3),
                  pl.BlockSpec((N_FF_CHUNKS, FF_CHUNK, d), c3, pipeline_mode=pl.Buffered(1))],
        out_specs=pl.BlockSpec((tm, d), row),
        out_shape=jax.ShapeDtypeStruct((n, d), F32),
        scratch_shapes=[pltpu.VMEM((tm, d), F32), pltpu.VMEM((tm + HALO, FF_CHUNK), F32),
                        pltpu.VMEM((nc2, HALO, FF_CHUNK), F32)],
        compiler_params=_cparams(1), name="conv_ffn",
    )(x2, ffn_norm.reshape(1, d), wup, cw, cb, wdn)


def _ple_kernel(x_ref, p_ref, g_ref, wg_ref, wp_ref, gf_ref, y_ref, *, final):
    x = x_ref[...]
    gate = jax.nn.sigmoid(_dot(_rms(x, g_ref[...]).astype(BF16), wg_ref[...]))
    x = x + gate * _dot(p_ref[...].astype(BF16), wp_ref[...])
    y_ref[...] = _rms(x, gf_ref[...]) if final else x


def _ple_final(x2, p2, ple_norm, w_gate, w_proj, final_norm, final, tm):
    n, d = x2.shape
    pd = p2.shape[1]
    row = lambda i: (i, 0)
    const = lambda i: (0, 0)
    return pl.pallas_call(
        functools.partial(_ple_kernel, final=final),
        grid=(n // tm,),
        in_specs=[pl.BlockSpec((tm, d), row), pl.BlockSpec((tm, pd), row), pl.BlockSpec((1, d), const),
                  pl.BlockSpec((d, d), const), pl.BlockSpec((pd, d), const), pl.BlockSpec((1, d), const)],
        out_specs=pl.BlockSpec((tm, d), row),
        out_shape=jax.ShapeDtypeStruct((n, d), F32),
        compiler_params=_cparams(1), name="ple_final_norm",
    )(x2, p2, ple_norm.reshape(1, d), w_gate.astype(BF16), w_proj.astype(BF16), final_norm.reshape(1, d))


def kernel(x, p, positions, attn_norm, w_in, w_o, ffn_norm, w_up, conv_w, conv_b, w_down,
           ple_norm, w_ple_gate, w_ple_proj, final_norm):
    b, t, d = x.shape
    depth = p.shape[0]
    n = b * t
    topk = min(TOPK_MAX, t // 4)
    assert t % DILATED_PAIRS[-1][0] == 0 and t % DSA_TQ == 0
    tm = 256
    x2 = x.reshape(n, d)
    pos2 = positions.reshape(n, 1)
    for i in range(depth):
        za, zb, zi, wi = _project(x2, pos2, attn_norm[i], w_in[i], 2 * tm)
        branches = [_dilated_branch(za, b, t, dil) for _, dil in DILATED_PAIRS]
        ob = _sparse_attention(zb, zi, wi, b, t, topk)
        x2 = _out_proj(x2, branches, ob, w_o[i], tm)
        x2 = _ffn(x2, t, ffn_norm[i], w_up[i], conv_w[i], conv_b[i], w_down[i], 2 * tm)
        x2 = _ple_final(x2, p[i].reshape(n, -1), ple_norm[i], w_ple_gate[i], w_ple_proj[i],
                        final_norm, i == depth - 1, tm)
    return x2.reshape(b, t, d)
```

```python
import functools
import math

import numpy as np
import jax
import jax.numpy as jnp
from jax import lax
from jax.experimental import pallas as pl
from jax.experimental.pallas import tpu as pltpu

F32 = jnp.float32
BF16 = jnp.bfloat16

HEAD_DIM = 64
HALF = HEAD_DIM // 2
N_HEADS = 8
GROUP_W = N_HEADS * HEAD_DIM
N_PAIRS = N_HEADS // 2
LANES = 128
DILATED_PAIRS = ((128, 1), (512, 4), (2048, 16))
WIN_STEPS = 128
TOPK_MAX = 256
D_FF = 2816
FF_CHUNK = 256
N_FF_CHUNKS = D_FF // FF_CHUNK
CONV_WIDTH = 3
ROPE_THETA = 10000.0
RMS_EPS = 1e-6
MASK_VALUE = -1e30
KEY_NEG_INF = -2139095041
VMEM_LIMIT = 56 * 1024 * 1024


def _cparams(n_axes):
    return pltpu.CompilerParams(dimension_semantics=("arbitrary",) * n_axes,
                                vmem_limit_bytes=VMEM_LIMIT)


def _dot_t(a, b):
    return lax.dot_general(a, b, (((1,), (1,)), ((), ())), preferred_element_type=F32)


def _dot(a, b):
    return jnp.dot(a, b, preferred_element_type=F32)


def _rms(x, g):
    ms = jnp.mean(x * x, axis=-1, keepdims=True)
    return (x * lax.rsqrt(ms + RMS_EPS)) * g


def _pair_perm():
    idx = []
    for j in range(N_PAIRS):
        a, b = 2 * j * HEAD_DIM, (2 * j + 1) * HEAD_DIM
        idx += list(range(a, a + HALF)) + list(range(b, b + HALF))
        idx += list(range(a + HALF, a + HEAD_DIM)) + list(range(b + HALF, b + HEAD_DIM))
    return np.asarray(idx, np.int32)


def _in_proj_columns():
    perm = _pair_perm()
    nat = np.arange(GROUP_W, dtype=np.int32)
    cols = []
    for g, roped in enumerate((True, True, False, True, True, False, True)):
        cols.append(g * GROUP_W + (perm if roped else nat))
    kbase = 7 * GROUP_W
    k1 = kbase + np.arange(HALF, dtype=np.int32)
    k2 = kbase + HALF + np.arange(HALF, dtype=np.int32)
    cols.append(np.concatenate([k1, k1, k2, k2]))
    return np.concatenate(cols)


PERM_ROWS = 256


def _residue_perm(dil):
    p = np.zeros((PERM_ROWS, PERM_ROWS), np.float32)
    per = PERM_ROWS // dil
    for r in range(dil):
        for u in range(per):
            p[r * per + u, u * dil + r] = 1.0
    return p


def _proj_kernel(x_ref, pos_ref, g_ref, invf_ref, w_ref, wiw_ref, p4_ref, p16_ref,
                 za_ref, zb_ref, zi_ref, wi_ref, za4_ref, za16_ref):
    x = x_ref[...]
    h32 = _rms(x, g_ref[...])
    h = h32.astype(BF16)
    ang = pos_ref[...].astype(F32) * invf_ref[...]
    lane = lax.broadcasted_iota(jnp.int32, (1, LANES), 1)
    cos = jnp.cos(ang)
    sin = jnp.sin(ang) * jnp.where(lane < 2 * HALF, -1.0, 1.0)

    def rope(z):
        return z * cos + pltpu.roll(z, 2 * HALF, 1) * sin

    q_scale = HEAD_DIM ** -0.5
    plan = ((za_ref, 0, True, q_scale), (za_ref, 1, True, 1.0), (za_ref, 2, False, 1.0),
            (zb_ref, 0, True, q_scale * math.log2(math.e)), (zb_ref, 1, True, 1.0), (zb_ref, 2, False, 1.0),
            (zi_ref, 0, True, q_scale))
    for g, (out_ref, slot, roped, scale) in enumerate(plan):
        z = _dot(h, w_ref[:, g * GROUP_W:(g + 1) * GROUP_W])
        for j in range(N_PAIRS):
            zj = z[:, j * LANES:(j + 1) * LANES]
            if roped:
                zj = rope(zj)
            if scale != 1.0:
                zj = zj * scale
            out_ref[:, slot * GROUP_W + j * LANES: slot * GROUP_W + (j + 1) * LANES] = zj.astype(BF16)
    zk = _dot(h, w_ref[:, 7 * GROUP_W: 7 * GROUP_W + LANES])
    zi_ref[:, GROUP_W:GROUP_W + LANES] = rope(zk).astype(BF16)
    wi_ref[...] = _dot(h, wiw_ref[...]) * (N_HEADS ** -0.5)
    for half in range(x.shape[0] // PERM_ROWS):
        za = za_ref[half * PERM_ROWS:(half + 1) * PERM_ROWS, :]
        for dil, p_ref, out_ref in ((4, p4_ref, za4_ref), (16, p16_ref, za16_ref)):
            per = PERM_ROWS // dil
            zp = _dot(p_ref[...], za).astype(BF16)
            for r in range(dil):
                out_ref[r, half * per:(half + 1) * per, :] = zp[r * per:(r + 1) * per, :]


def _project(x2, pos2, attn_norm, w_in, b, t, tm):
    n, d = x2.shape
    tiles = t // tm
    res_map = lambda i: (i // tiles, 0, i % tiles, 0)
    cols = _in_proj_columns()
    w_main = jnp.take(w_in, cols, axis=1).astype(BF16)
    w_wi = jnp.pad(w_in[:, 7 * GROUP_W + HEAD_DIM:], ((0, 0), (0, LANES - N_HEADS))).astype(BF16)
    inv_freq = 1.0 / (ROPE_THETA ** (np.arange(0, HEAD_DIM, 2, dtype=np.float32) / HEAD_DIM))
    invf = jnp.asarray(np.tile(inv_freq, 4)[None, :], F32)
    wcols = w_main.shape[1]
    const = lambda i: (0, 0)
    row = lambda i: (i, 0)
    return pl.pallas_call(
        _proj_kernel,
        grid=(n // tm,),
        in_specs=[pl.BlockSpec((tm, d), row), pl.BlockSpec((tm, 1), row),
                  pl.BlockSpec((1, d), const), pl.BlockSpec((1, LANES), const),
                  pl.BlockSpec((d, wcols), const, pipeline_mode=pl.Buffered(1)),
                  pl.BlockSpec((d, LANES), const),
                  pl.BlockSpec((PERM_ROWS, PERM_ROWS), const),
                  pl.BlockSpec((PERM_ROWS, PERM_ROWS), const)],
        out_specs=[pl.BlockSpec((tm, 3 * GROUP_W), row), pl.BlockSpec((tm, 3 * GROUP_W), row),
                   pl.BlockSpec((tm, GROUP_W + LANES), row), pl.BlockSpec((tm, LANES), row),
                   pl.BlockSpec((None, 4, tm // 4, 3 * GROUP_W), res_map),
                   pl.BlockSpec((None, 16, tm // 16, 3 * GROUP_W), res_map)],
        out_shape=[jax.ShapeDtypeStruct((n, 3 * GROUP_W), BF16),
                   jax.ShapeDtypeStruct((n, 3 * GROUP_W), BF16),
                   jax.ShapeDtypeStruct((n, GROUP_W + LANES), BF16),
                   jax.ShapeDtypeStruct((n, LANES), F32),
                   jax.ShapeDtypeStruct((b, 4, t // 4, 3 * GROUP_W), BF16),
                   jax.ShapeDtypeStruct((b, 16, t // 16, 3 * GROUP_W), BF16)],
        compiler_params=_cparams(1), name="in_proj_rope",
    )(x2, pos2, attn_norm.reshape(1, d), invf, w_main, w_wi,
      jnp.asarray(_residue_perm(4), BF16), jnp.asarray(_residue_perm(16), BF16))


def _head_lane_masks():
    lane = lax.broadcasted_iota(jnp.int32, (1, LANES), 1)
    first = (lane % HEAD_DIM) < HALF
    return first, jnp.logical_not(first)


def _dilated_kernel(q_ref, kc_ref, kp_ref, vc_ref, vp_ref, o_ref, lse_ref):
    nb = pl.program_id(2)
    n = WIN_STEPS
    a = lax.broadcasted_iota(jnp.int32, (n, 2 * n), 0)
    c = lax.broadcasted_iota(jnp.int32, (n, 2 * n), 1)
    band = (c >= a) & (c <= a + n) & ((nb > 0) | (c >= n))
    masks = _head_lane_masks()
    lane = lax.broadcasted_iota(jnp.int32, (1, LANES), 1)
    low = lane < HEAD_DIM
    for j in range(N_PAIRS):
        sl = slice(j * LANES, (j + 1) * LANES)
        q = q_ref[:, sl]
        kk = jnp.concatenate([kp_ref[:, sl], kc_ref[:, sl]], axis=0)
        vv = jnp.concatenate([vp_ref[:, sl], vc_ref[:, sl]], axis=0)
        outs, lses = [], []
        for hb in range(2):
            qm = jnp.where(masks[hb], q, jnp.zeros_like(q))
            s = _dot_t(qm, kk)
            s = jnp.where(band, s, -jnp.inf)
            m = jnp.max(s, axis=1, keepdims=True)
            p = jnp.exp(s - m)
            l = jnp.sum(p, axis=1, keepdims=True)
            outs.append(_dot(p.astype(BF16), vv) / l)
            lses.append(m + jnp.log(l))
        o_ref[:, sl] = jnp.where(low, outs[0], outs[1])
        lse_ref[:, sl] = jnp.where(low, lses[0], lses[1])


def _dilated_branch(zr, dil):
    n = WIN_STEPS
    b, _, rows, _ = zr.shape
    blk = (None, None, n, GROUP_W)
    cur = lambda slot: (lambda bi, r, nb: (bi, r, nb, slot))
    prev = lambda slot: (lambda bi, r, nb: (bi, r, jnp.maximum(nb - 1, 0), slot))
    return pl.pallas_call(
        _dilated_kernel,
        grid=(b, dil, rows // n),
        in_specs=[pl.BlockSpec(blk, cur(0)), pl.BlockSpec(blk, cur(1)), pl.BlockSpec(blk, prev(1)),
                  pl.BlockSpec(blk, cur(2)), pl.BlockSpec(blk, prev(2))],
        out_specs=[pl.BlockSpec(blk, cur(0)), pl.BlockSpec(blk, cur(0))],
        out_shape=[jax.ShapeDtypeStruct((b, dil, rows, GROUP_W), F32)] * 2,
        compiler_params=_cparams(3), name=f"dilated_attn_d{dil}",
    )(zr, zr, zr, zr, zr)


DSA_TQ = 256
P1_CHUNK = 256
P3_CHUNK = 256
CNT_ROWS = 32
BF16_ROWS = 16
VT_ROWS = HEAD_DIM + BF16_ROWS
BF16_MIN_NORMAL = 0x0080
CHECK_EVERY = 4


def _to_key(x):
    bits = pltpu.bitcast(x, jnp.int32)
    return bits ^ ((bits >> 31) & 0x7FFFFFFF)


def _dsa_kernel(qb_ref, qi_ref, wit_ref, kb_ref, vt_ref, ki_ref, ot_ref,
                key_ref, top_ref, qim_ref, qbm_ref, m_ref, acc_ref, thr_ref, s_ref, mx_ref,
                *, topk):
    assert P3_CHUNK == P1_CHUNK == DSA_TQ
    tq = DSA_TQ
    qblk = pl.program_id(1)
    t0 = qblk * tq
    n1 = qblk + 1
    masks = _head_lane_masks()

    for h in range(N_HEADS):
        j, hb = divmod(h, 2)
        sl = slice(j * LANES, (j + 1) * LANES)
        qi = qi_ref[:, sl]
        qb = qb_ref[:, sl]
        qim_ref[h] = jnp.where(masks[hb], qi, jnp.zeros_like(qi))
        qbm_ref[h] = jnp.where(masks[hb], qb, jnp.zeros_like(qb))

    def p1_chunk(c, diagonal):
        k0 = pl.multiple_of(c * P1_CHUNK, P1_CHUNK)
        kic = ki_ref[pl.ds(k0, P1_CHUNK), :]
        acc = jnp.zeros((P1_CHUNK, tq), F32)
        for h in range(N_HEADS):
            acc = acc + wit_ref[h:h + 1, :] * jnp.maximum(_dot_t(kic, qim_ref[h]), 0.0)
        acc = jnp.where(acc == 0.0, 0.0, acc)
        if diagonal:
            kpos = lax.broadcasted_iota(jnp.int32, (P1_CHUNK, tq), 0)
            qpos = lax.broadcasted_iota(jnp.int32, (P1_CHUNK, tq), 1)
            acc = jnp.where(kpos <= qpos, acc, -jnp.inf)
        key_ref[pl.ds(k0, P1_CHUNK), :] = _to_key(acc)
        top = pltpu.bitcast(pltpu.bitcast(acc, jnp.int32) & (-65536), F32)
        top_ref[pl.ds(k0, P1_CHUNK), :] = top.astype(BF16)

    def p1_pair(i, carry):
        p1_chunk(2 * i, False)
        p1_chunk(2 * i + 1, False)
        return carry

    lax.fori_loop(0, qblk // 2, p1_pair, 0)

    @pl.when(qblk % 2 == 1)
    def _():
        p1_chunk(qblk - 1, False)

    p1_chunk(qblk, True)

    def count_ge(thr):
        def body(c, cnt):
            k0 = pl.multiple_of(c * P1_CHUNK, P1_CHUNK)
            x = jnp.where(key_ref[pl.ds(k0, P1_CHUNK), :] >= thr, 1, 0)
            return cnt + jnp.sum(x.reshape(P1_CHUNK // CNT_ROWS, CNT_ROWS, tq), axis=0)

        cnt = lax.fori_loop(0, n1, body, jnp.zeros((CNT_ROWS, tq), jnp.int32))
        return jnp.sum(cnt, axis=0, keepdims=True)

    def count_ge_top(mid16):
        pattern = mid16 ^ ((mid16 >> 15) & 0x7FFF)
        pattern = jnp.where((mid16 > 0) & (mid16 < BF16_MIN_NORMAL), BF16_MIN_NORMAL, pattern)
        thr_f = pltpu.bitcast(pattern << 16, F32)
        thr_b = jnp.broadcast_to(thr_f, (BF16_ROWS, tq)).astype(BF16)[None]
        one, zero = jnp.ones((), BF16), jnp.zeros((), BF16)

        def body(c, cnt):
            k0 = pl.multiple_of(c * P1_CHUNK, P1_CHUNK)
            x = top_ref[pl.ds(k0, P1_CHUNK), :].reshape(P1_CHUNK // BF16_ROWS, BF16_ROWS, tq)
            y = jnp.where(x >= thr_b, one, zero)
            parts = [y[r] for r in range(P1_CHUNK // BF16_ROWS)]
            while len(parts) > 1:
                parts = [a + b for a, b in zip(parts[::2], parts[1::2])]
            return cnt + parts[0].astype(F32)

        cnt = lax.fori_loop(0, n1, body, jnp.zeros((BF16_ROWS, tq), F32))
        return jnp.sum(cnt, axis=0, keepdims=True).astype(jnp.int32)

    def step(st, count_fn, final):
        lo, hi, thr, done, cnt_hi = st
        mid = (lo >> 1) + (hi >> 1) + (((lo & 1) + (hi & 1) + 1) >> 1)
        c = count_fn(mid)
        ge = c >= topk
        active = done == 0
        hit = active & (c == topk)
        new_lo = jnp.where(active & ge, mid, lo)
        new_hi = jnp.where(active & jnp.logical_not(ge), mid - 1, hi)
        new_cnt_hi = jnp.where(active & jnp.logical_not(ge), c, cnt_hi)
        new_thr = jnp.where(hit, mid, thr)
        new_done = jnp.where(hit, 1, done)
        if final:
            conv = active & jnp.logical_not(hit) & (new_lo == new_hi)
            new_thr = jnp.where(conv, new_lo, new_thr)
            new_done = jnp.where(conv, 2, new_done)
        return new_lo, new_hi, new_thr, new_done, new_cnt_hi

    t_row = t0 + lax.broadcasted_iota(jnp.int32, (1, tq), 1)
    all_sel = t_row + 1 <= topk
    done0 = all_sel.astype(jnp.int32)
    zeros = jnp.zeros((1, tq), jnp.int32)
    lo16 = jnp.full((1, tq), (KEY_NEG_INF + 1) >> 16, jnp.int32)
    hi16 = jnp.full((1, tq), 0x7F7F, jnp.int32)
    st = lax.fori_loop(0, 16, lambda _, s: step(s, count_ge_top, False),
                       (lo16, hi16, zeros, done0, zeros))
    lo16, _, thr16, done, cnt_hi = st
    lo = lo16 << 16
    thr = jnp.where(all_sel, KEY_NEG_INF + 1, thr16 << 16)

    def fine_cond(carry):
        it, st = carry
        return (it < 16 // CHECK_EVERY) & (jnp.min(st[3]) == 0)

    def fine_body(carry):
        it, st = carry
        return it + 1, lax.fori_loop(0, CHECK_EVERY, lambda _, s: step(s, count_ge, True), st)

    _, st = lax.while_loop(fine_cond, fine_body, (0, (lo, lo + 0xFFFF, thr, done, cnt_hi)))
    _, _, thr, done, cnt_hi = st
    thr_ref[...] = thr

    tie = done == 2

    @pl.when(jnp.max(done) == 2)
    def _():
        need = topk - cnt_hi
        sub_i = lax.broadcasted_iota(jnp.int32, (P1_CHUNK, tq), 0)

        def count_eq_le(jmax):
            def body(c, cnt):
                k0 = pl.multiple_of(c * P1_CHUNK, P1_CHUNK)
                kc = key_ref[pl.ds(k0, P1_CHUNK), :]
                x = jnp.where((kc == thr) & (sub_i + k0 <= jmax), 1, 0)
                return cnt + jnp.sum(x.reshape(P1_CHUNK // CNT_ROWS, CNT_ROWS, tq), axis=0)

            cnt = lax.fori_loop(0, n1, body, jnp.zeros((CNT_ROWS, tq), jnp.int32))
            return jnp.sum(cnt, axis=0, keepdims=True)

        def jbody(_, st):
            jlo, jhi = st
            jmid = (jlo + jhi) >> 1
            ok = count_eq_le(jmid) >= need
            return jnp.where(ok, jlo, jmid + 1), jnp.where(ok, jmid, jhi)

        jlo, _ = lax.fori_loop(0, 14, jbody, (jnp.zeros((1, tq), jnp.int32),
                                               jnp.broadcast_to(t0 + tq - 1, (1, tq)).astype(jnp.int32)))
        jcut = jnp.where(tie, jlo, 0x7FFFFFFF)

        def demote(c, carry):
            k0 = pl.multiple_of(c * P1_CHUNK, P1_CHUNK)
            kc = key_ref[pl.ds(k0, P1_CHUNK), :]
            key_ref[pl.ds(k0, P1_CHUNK), :] = jnp.where((kc == thr) & (sub_i + k0 > jcut), kc - 1, kc)
            return carry

        lax.fori_loop(0, n1, demote, 0)

    m_ref[...] = jnp.full(m_ref.shape, MASK_VALUE, F32)
    acc_ref[...] = jnp.zeros(acc_ref.shape, F32)

    def scores(c, slot):
        k0 = pl.multiple_of(c * P3_CHUNK, P3_CHUNK)
        sel = key_ref[pl.ds(k0, P3_CHUNK), :] >= thr_ref[...]
        for h in range(N_HEADS):
            j = h // 2
            kc = kb_ref[pl.ds(k0, P3_CHUNK), j * LANES:(j + 1) * LANES]
            s = jnp.where(sel, _dot_t(kc, qbm_ref[h]), MASK_VALUE)
            s_ref[slot, h] = s
            mx_ref[slot, h] = jnp.max(s, axis=0, keepdims=True)

    def accumulate(c, slot):
        k0 = pl.multiple_of(c * P3_CHUNK, P3_CHUNK)
        for h in range(N_HEADS):
            rows = slice(h * VT_ROWS, (h + 1) * VT_ROWS)
            m_old = m_ref[h]
            m_new = jnp.maximum(m_old, mx_ref[slot, h])
            alpha = jnp.exp2(m_old - m_new)
            p = jnp.exp2((s_ref[slot, h] - m_new).astype(BF16))
            m_ref[h] = m_new
            pv = _dot(vt_ref[rows, pl.ds(k0, P3_CHUNK)], p)
            acc_ref[rows, :] = alpha * acc_ref[rows, :] + pv

    scores(0, 0)

    def p3_pair(i, carry):
        c = 2 * i
        scores(c + 1, 1)
        accumulate(c, 0)
        scores(c + 2, 0)
        accumulate(c + 1, 1)
        return carry

    lax.fori_loop(0, (n1 - 1) // 2, p3_pair, 0)

    @pl.when(n1 % 2 == 1)
    def _():
        accumulate(n1 - 1, 0)

    @pl.when(n1 % 2 == 0)
    def _():
        scores(n1 - 1, 1)
        accumulate(n1 - 2, 0)
        accumulate(n1 - 1, 1)

    for h in range(N_HEADS):
        base = h * VT_ROWS
        denom = acc_ref[base + HEAD_DIM:base + HEAD_DIM + 1, :]
        ot_ref[h * HEAD_DIM:(h + 1) * HEAD_DIM, :] = (acc_ref[base:base + HEAD_DIM, :] / denom).astype(BF16)


def _sparse_attention(zb, zi, wi, b, t, topk):
    tq = DSA_TQ
    zb3 = zb.reshape(b, t, 3 * GROUP_W)
    zi3 = zi.reshape(b, t, GROUP_W + LANES)
    vt = zb3[:, :, 2 * GROUP_W:].transpose(0, 2, 1).reshape(b, N_HEADS, HEAD_DIM, t)
    pad = jnp.zeros((b, N_HEADS, VT_ROWS - HEAD_DIM, t), BF16).at[:, :, 0, :].set(1.0)
    vt = jnp.concatenate([vt, pad], axis=2).reshape(b, N_HEADS * VT_ROWS, t)
    wit = wi.reshape(b, t, LANES)[:, :, :N_HEADS].transpose(0, 2, 1)
    qmap = lambda bi, qi: (bi, qi, 0)
    tmap = lambda bi, qi: (bi, 0, qi)
    out_t = pl.pallas_call(
        functools.partial(_dsa_kernel, topk=topk),
        grid=(b, t // tq),
        in_specs=[pl.BlockSpec((None, tq, GROUP_W), qmap),
                  pl.BlockSpec((None, tq, GROUP_W), qmap),
                  pl.BlockSpec((None, N_HEADS, tq), tmap),
                  pl.BlockSpec((None, t, GROUP_W), lambda bi, qi: (bi, 0, 1),
                               pipeline_mode=pl.Buffered(1)),
                  pl.BlockSpec((None, N_HEADS * VT_ROWS, t), lambda bi, qi: (bi, 0, 0),
                               pipeline_mode=pl.Buffered(1)),
                  pl.BlockSpec((None, t, LANES), lambda bi, qi: (bi, 0, GROUP_W // LANES),
                               pipeline_mode=pl.Buffered(1))],
        out_specs=pl.BlockSpec((None, GROUP_W, tq), tmap),
        out_shape=jax.ShapeDtypeStruct((b, GROUP_W, t), BF16),
        scratch_shapes=[pltpu.VMEM((t, tq), jnp.int32),
                        pltpu.VMEM((t, tq), BF16),
                        pltpu.VMEM((N_HEADS, tq, LANES), BF16),
                        pltpu.VMEM((N_HEADS, tq, LANES), BF16),
                        pltpu.VMEM((N_HEADS, 1, tq), F32),
                        pltpu.VMEM((N_HEADS * VT_ROWS, tq), F32),
                        pltpu.VMEM((1, tq), jnp.int32),
                        pltpu.VMEM((2, N_HEADS, P3_CHUNK, tq), F32),
                        pltpu.VMEM((2, N_HEADS, 1, tq), F32)],
        compiler_params=_cparams(2), name="dsa_attn",
    )(zb3, zi3, wit, zb3, vt, zi3)
    return out_t.transpose(0, 2, 1).reshape(b * t, GROUP_W)


def _to_token_order(r_ref, pt_ref):
    r = r_ref[...].reshape(PERM_ROWS, GROUP_W)
    hi = r.astype(BF16)
    lo = (r - hi.astype(F32)).astype(BF16)
    return _dot(pt_ref[...], hi) + _dot(pt_ref[...], lo)


def _out_proj_kernel(x_ref, o1_ref, l1_ref, o4_ref, l4_ref, o16_ref, l16_ref, ob_ref, wo_ref,
                     pt4_ref, pt16_ref, y_ref):
    o1, l1 = o1_ref[...], l1_ref[...]
    o2, l2 = _to_token_order(o4_ref, pt4_ref), _to_token_order(l4_ref, pt4_ref)
    o3, l3 = _to_token_order(o16_ref, pt16_ref), _to_token_order(l16_ref, pt16_ref)
    m = jnp.maximum(jnp.maximum(l1, l2), l3)
    e1, e2, e3 = jnp.exp(l1 - m), jnp.exp(l2 - m), jnp.exp(l3 - m)
    out_a = (e1 * o1 + e2 * o2 + e3 * o3) / (e1 + e2 + e3)
    y = x_ref[...] + _dot(out_a.astype(BF16), wo_ref[0:GROUP_W, :])
    y_ref[...] = y + _dot(ob_ref[...], wo_ref[GROUP_W:2 * GROUP_W, :])


def _out_proj(x2, branches, ob, w_o, t):
    n, d = x2.shape
    tm = PERM_ROWS
    tiles = t // tm
    row = lambda i: (i, 0)
    const = lambda i: (0, 0)
    res_map = lambda i: (i // tiles, 0, i % tiles, 0)
    gspec = pl.BlockSpec((tm, GROUP_W), row)
    rspec = lambda dil: pl.BlockSpec((None, dil, tm // dil, GROUP_W), res_map)
    pspec = pl.BlockSpec((tm, tm), const)
    (o1, l1), (o4, l4), (o16, l16) = branches
    return pl.pallas_call(
        _out_proj_kernel,
        grid=(n // tm,),
        in_specs=[pl.BlockSpec((tm, d), row), gspec, gspec, rspec(4), rspec(4), rspec(16), rspec(16),
                  gspec, pl.BlockSpec((2 * GROUP_W, d), const), pspec, pspec],
        out_specs=pl.BlockSpec((tm, d), row),
        out_shape=jax.ShapeDtypeStruct((n, d), F32),
        compiler_params=_cparams(1), name="merge_out_proj",
    )(x2, o1.reshape(n, GROUP_W), l1.reshape(n, GROUP_W), o4, l4, o16, l16, ob, w_o.astype(BF16),
      jnp.asarray(_residue_perm(4).T, BF16), jnp.asarray(_residue_perm(16).T, BF16))


HALO = 8


def _ffn_kernel(x_ref, g_ref, wup_ref, cw_ref, cb_ref, wdn_ref, y_ref,
                acc_ref, buf_ref, tail_ref, *, tiles_per_seq):
    tm = x_ref.shape[0]
    i = pl.program_id(0)

    @pl.when(i % tiles_per_seq == 0)
    def _():
        tail_ref[...] = jnp.zeros(tail_ref.shape, F32)

    x = x_ref[...]
    h = _rms(x, g_ref[...]).astype(BF16)
    acc_ref[...] = x

    def conv(c):
        u = _dot(h, wup_ref[c])
        buf_ref[0:HALO, :] = tail_ref[c]
        buf_ref[HALO:HALO + tm, :] = u
        tail_ref[c] = u[tm - HALO:, :]
        w = cw_ref[c]
        y = cb_ref[c] + w[2:3, :] * u
        y = y + w[1:2, :] * buf_ref[HALO - 1:HALO - 1 + tm, :]
        return y + w[0:1, :] * buf_ref[HALO - 2:HALO - 2 + tm, :]

    def body(c, carry):
        gate = conv(c)
        up = conv(c + N_FF_CHUNKS)
        act = (gate * jax.nn.sigmoid(gate) * up).astype(BF16)
        acc_ref[...] += _dot(act, wdn_ref[c])
        return carry

    lax.fori_loop(0, N_FF_CHUNKS, body, 0, unroll=True)
    y_ref[...] = acc_ref[...]


def _ffn(x2, t, ffn_norm, w_up, conv_w, conv_b, w_down, tm):
    n, d = x2.shape
    nc2 = 2 * N_FF_CHUNKS
    wup = w_up.reshape(d, nc2, FF_CHUNK).transpose(1, 0, 2).astype(BF16)
    cw = conv_w.reshape(CONV_WIDTH, nc2, FF_CHUNK).transpose(1, 0, 2)
    cb = conv_b.reshape(nc2, 1, FF_CHUNK)
    wdn = w_down.reshape(N_FF_CHUNKS, FF_CHUNK, d).astype(BF16)
    row = lambda i: (i, 0)
    c3 = lambda i: (0, 0, 0)
    return pl.pallas_call(
        functools.partial(_ffn_kernel, tiles_per_seq=t // tm),
        grid=(n // tm,),
        in_specs=[pl.BlockSpec((tm, d), row), pl.BlockSpec((1, d), lambda i: (0, 0)),
                  pl.BlockSpec((nc2, d, FF_CHUNK), c3, pipeline_mode=pl.Buffered(1)),
                  pl.BlockSpec((nc2, CONV_WIDTH, FF_CHUNK), c3),
                  pl.BlockSpec((nc2, 1, FF_CHUNK), c3),
                  pl.BlockSpec((N_FF_CHUNKS, FF_CHUNK, d), c3, pipeline_mode=pl.Buffered(1))],
        out_specs=pl.BlockSpec((tm, d), row),
        out_shape=jax.ShapeDtypeStruct((n, d), F32),
        scratch_shapes=[pltpu.VMEM((tm, d), F32), pltpu.VMEM((tm + HALO, FF_CHUNK), F32),
                        pltpu.VMEM((nc2, HALO, FF_CHUNK), F32)],
        compiler_params=_cparams(1), name="conv_ffn",
    )(x2, ffn_norm.reshape(1, d), wup, cw, cb, wdn)


def _ple_kernel(x_ref, p_ref, g_ref, wg_ref, wp_ref, gf_ref, y_ref, *, final):
    x = x_ref[...]
    gate = jax.nn.sigmoid(_dot(_rms(x, g_ref[...]).astype(BF16), wg_ref[...]))
    x = x + gate * _dot(p_ref[...].astype(BF16), wp_ref[...])
    y_ref[...] = _rms(x, gf_ref[...]) if final else x


def _ple_final(x2, p2, ple_norm, w_gate, w_proj, final_norm, final, tm):
    n, d = x2.shape
    pd = p2.shape[1]
    row = lambda i: (i, 0)
    const = lambda i: (0, 0)
    return pl.pallas_call(
        functools.partial(_ple_kernel, final=final),
        grid=(n // tm,),
        in_specs=[pl.BlockSpec((tm, d), row), pl.BlockSpec((tm, pd), row), pl.BlockSpec((1, d), const),
                  pl.BlockSpec((d, d), const), pl.BlockSpec((pd, d), const), pl.BlockSpec((1, d), const)],
        out_specs=pl.BlockSpec((tm, d), row),
        out_shape=jax.ShapeDtypeStruct((n, d), F32),
        compiler_params=_cparams(1), name="ple_final_norm",
    )(x2, p2, ple_norm.reshape(1, d), w_gate.astype(BF16), w_proj.astype(BF16), final_norm.reshape(1, d))


def kernel(x, p, positions, attn_norm, w_in, w_o, ffn_norm, w_up, conv_w, conv_b, w_down,
           ple_norm, w_ple_gate, w_ple_proj, final_norm):
    b, t, d = x.shape
    depth = p.shape[0]
    n = b * t
    topk = min(TOPK_MAX, t // 4)
    assert t % DILATED_PAIRS[-1][0] == 0 and t % DSA_TQ == 0
    tm = 256
    x2 = x.reshape(n, d)
    pos2 = positions.reshape(n, 1)
    for i in range(depth):
        za, zb, zi, wi, za4, za16 = _project(x2, pos2, attn_norm[i], w_in[i], b, t, 2 * tm)
        views = (za.reshape(b, 1, t, 3 * GROUP_W), za4, za16)
        branches = [_dilated_branch(zr, dil) for zr, (_, dil) in zip(views, DILATED_PAIRS)]
        ob = _sparse_attention(zb, zi, wi, b, t, topk)
        x2 = _out_proj(x2, branches, ob, w_o[i], t)
        x2 = _ffn(x2, t, ffn_norm[i], w_up[i], conv_w[i], conv_b[i], w_down[i], 2 * tm)
        x2 = _ple_final(x2, p[i].reshape(n, -1), ple_norm[i], w_ple_gate[i], w_ple_proj[i],
                        final_norm, i == depth - 1, tm)
    return x2.reshape(b, t, d)
```

```python
import functools
import math

import numpy as np
import jax
import jax.numpy as jnp
from jax import lax
from jax.experimental import pallas as pl
from jax.experimental.pallas import tpu as pltpu

F32 = jnp.float32
BF16 = jnp.bfloat16

HEAD_DIM = 64
HALF = HEAD_DIM // 2
N_HEADS = 8
GROUP_W = N_HEADS * HEAD_DIM
N_PAIRS = N_HEADS // 2
LANES = 128
DILATED_PAIRS = ((128, 1), (512, 4), (2048, 16))
WIN_STEPS = 128
TOPK_MAX = 256
D_FF = 2816
FF_CHUNK = 256
N_FF_CHUNKS = D_FF // FF_CHUNK
CONV_WIDTH = 3
ROPE_THETA = 10000.0
RMS_EPS = 1e-6
MASK_VALUE = -1e30
KEY_NEG_INF = -2139095041
VMEM_LIMIT = 56 * 1024 * 1024


def _cparams(n_axes):
    return pltpu.CompilerParams(dimension_semantics=("arbitrary",) * n_axes,
                                vmem_limit_bytes=VMEM_LIMIT)


def _dot_t(a, b):
    return lax.dot_general(a, b, (((1,), (1,)), ((), ())), preferred_element_type=F32)


def _dot(a, b):
    return jnp.dot(a, b, preferred_element_type=F32)


def _rms(x, g):
    ms = jnp.mean(x * x, axis=-1, keepdims=True)
    return (x * lax.rsqrt(ms + RMS_EPS)) * g


def _pair_perm():
    idx = []
    for j in range(N_PAIRS):
        a, b = 2 * j * HEAD_DIM, (2 * j + 1) * HEAD_DIM
        idx += list(range(a, a + HALF)) + list(range(b, b + HALF))
        idx += list(range(a + HALF, a + HEAD_DIM)) + list(range(b + HALF, b + HEAD_DIM))
    return np.asarray(idx, np.int32)


def _in_proj_columns():
    perm = _pair_perm()
    nat = np.arange(GROUP_W, dtype=np.int32)
    cols = []
    for g, roped in enumerate((True, True, False, True, True, False, True)):
        cols.append(g * GROUP_W + (perm if roped else nat))
    kbase = 7 * GROUP_W
    k1 = kbase + np.arange(HALF, dtype=np.int32)
    k2 = kbase + HALF + np.arange(HALF, dtype=np.int32)
    cols.append(np.concatenate([k1, k1, k2, k2]))
    return np.concatenate(cols)


PERM_ROWS = 256


def _residue_perm(dil):
    p = np.zeros((PERM_ROWS, PERM_ROWS), np.float32)
    per = PERM_ROWS // dil
    for r in range(dil):
        for u in range(per):
            p[r * per + u, u * dil + r] = 1.0
    return p


def _proj_kernel(x_ref, pos_ref, g_ref, invf_ref, w_ref, wiw_ref, p4_ref, p16_ref,
                 za_ref, zb_ref, zi_ref, wi_ref, za4_ref, za16_ref):
    x = x_ref[...]
    h32 = _rms(x, g_ref[...])
    h = h32.astype(BF16)
    ang = pos_ref[...].astype(F32) * invf_ref[...]
    lane = lax.broadcasted_iota(jnp.int32, (1, LANES), 1)
    cos = jnp.cos(ang)
    sin = jnp.sin(ang) * jnp.where(lane < 2 * HALF, -1.0, 1.0)

    def rope(z):
        return z * cos + pltpu.roll(z, 2 * HALF, 1) * sin

    q_scale = HEAD_DIM ** -0.5
    plan = ((za_ref, 0, True, q_scale), (za_ref, 1, True, 1.0), (za_ref, 2, False, 1.0),
            (zb_ref, 0, True, q_scale * math.log2(math.e)), (zb_ref, 1, True, 1.0), (zb_ref, 2, False, 1.0),
            (zi_ref, 0, True, q_scale))
    for g, (out_ref, slot, roped, scale) in enumerate(plan):
        z = _dot(h, w_ref[:, g * GROUP_W:(g + 1) * GROUP_W])
        for j in range(N_PAIRS):
            zj = z[:, j * LANES:(j + 1) * LANES]
            if roped:
                zj = rope(zj)
            if scale != 1.0:
                zj = zj * scale
            out_ref[:, slot * GROUP_W + j * LANES: slot * GROUP_W + (j + 1) * LANES] = zj.astype(BF16)
    zk = _dot(h, w_ref[:, 7 * GROUP_W: 7 * GROUP_W + LANES])
    zi_ref[:, GROUP_W:GROUP_W + LANES] = rope(zk).astype(BF16)
    wi_ref[...] = _dot(h, wiw_ref[...]) * (N_HEADS ** -0.5)
    for half in range(x.shape[0] // PERM_ROWS):
        za = za_ref[half * PERM_ROWS:(half + 1) * PERM_ROWS, :]
        for dil, p_ref, out_ref in ((4, p4_ref, za4_ref), (16, p16_ref, za16_ref)):
            per = PERM_ROWS // dil
            zp = _dot(p_ref[...], za).astype(BF16)
            for r in range(dil):
                out_ref[r, half * per:(half + 1) * per, :] = zp[r * per:(r + 1) * per, :]


def _project(x2, pos2, attn_norm, w_in, b, t, tm):
    n, d = x2.shape
    tiles = t // tm
    res_map = lambda i: (i // tiles, 0, i % tiles, 0)
    cols = _in_proj_columns()
    w_main = jnp.take(w_in, cols, axis=1).astype(BF16)
    w_wi = jnp.pad(w_in[:, 7 * GROUP_W + HEAD_DIM:], ((0, 0), (0, LANES - N_HEADS))).astype(BF16)
    inv_freq = 1.0 / (ROPE_THETA ** (np.arange(0, HEAD_DIM, 2, dtype=np.float32) / HEAD_DIM))
    invf = jnp.asarray(np.tile(inv_freq, 4)[None, :], F32)
    wcols = w_main.shape[1]
    const = lambda i: (0, 0)
    row = lambda i: (i, 0)
    return pl.pallas_call(
        _proj_kernel,
        grid=(n // tm,),
        in_specs=[pl.BlockSpec((tm, d), row), pl.BlockSpec((tm, 1), row),
                  pl.BlockSpec((1, d), const), pl.BlockSpec((1, LANES), const),
                  pl.BlockSpec((d, wcols), const, pipeline_mode=pl.Buffered(1)),
                  pl.BlockSpec((d, LANES), const),
                  pl.BlockSpec((PERM_ROWS, PERM_ROWS), const),
                  pl.BlockSpec((PERM_ROWS, PERM_ROWS), const)],
        out_specs=[pl.BlockSpec((tm, 3 * GROUP_W), row), pl.BlockSpec((tm, 3 * GROUP_W), row),
                   pl.BlockSpec((tm, GROUP_W + LANES), row), pl.BlockSpec((tm, LANES), row),
                   pl.BlockSpec((None, 4, tm // 4, 3 * GROUP_W), res_map),
                   pl.BlockSpec((None, 16, tm // 16, 3 * GROUP_W), res_map)],
        out_shape=[jax.ShapeDtypeStruct((n, 3 * GROUP_W), BF16),
                   jax.ShapeDtypeStruct((n, 3 * GROUP_W), BF16),
                   jax.ShapeDtypeStruct((n, GROUP_W + LANES), BF16),
                   jax.ShapeDtypeStruct((n, LANES), F32),
                   jax.ShapeDtypeStruct((b, 4, t // 4, 3 * GROUP_W), BF16),
                   jax.ShapeDtypeStruct((b, 16, t // 16, 3 * GROUP_W), BF16)],
        compiler_params=_cparams(1), name="in_proj_rope",
    )(x2, pos2, attn_norm.reshape(1, d), invf, w_main, w_wi,
      jnp.asarray(_residue_perm(4), BF16), jnp.asarray(_residue_perm(16), BF16))


def _head_lane_masks():
    lane = lax.broadcasted_iota(jnp.int32, (1, LANES), 1)
    first = (lane % HEAD_DIM) < HALF
    return first, jnp.logical_not(first)


def _dilated_kernel(q_ref, kc_ref, kp_ref, vc_ref, vp_ref, o_ref, lse_ref, s_ref):
    nb = pl.program_id(2)
    n = WIN_STEPS
    a = lax.broadcasted_iota(jnp.int32, (n, 2 * n), 0)
    c = lax.broadcasted_iota(jnp.int32, (n, 2 * n), 1)
    band = (c >= a) & (c <= a + n) & ((nb > 0) | (c >= n))
    masks = _head_lane_masks()
    lane = lax.broadcasted_iota(jnp.int32, (1, LANES), 1)
    low = lane < HEAD_DIM
    for h in range(N_HEADS):
        j, hb = divmod(h, 2)
        sl = slice(j * LANES, (j + 1) * LANES)
        q = q_ref[:, sl]
        kk = jnp.concatenate([kp_ref[:, sl], kc_ref[:, sl]], axis=0)
        qm = jnp.where(masks[hb], q, jnp.zeros_like(q))
        s_ref[h] = jnp.where(band, _dot_t(qm, kk), -jnp.inf)
    for j in range(N_PAIRS):
        sl = slice(j * LANES, (j + 1) * LANES)
        vv = jnp.concatenate([vp_ref[:, sl], vc_ref[:, sl]], axis=0)
        outs, lses = [], []
        for hb in range(2):
            s = s_ref[2 * j + hb]
            m = jnp.max(s, axis=1, keepdims=True)
            p = jnp.exp(s - m)
            l = jnp.sum(p, axis=1, keepdims=True)
            outs.append(_dot(p.astype(BF16), vv) / l)
            lses.append(m + jnp.log(l))
        o_ref[:, sl] = jnp.where(low, outs[0], outs[1])
        lse_ref[:, sl] = jnp.where(low, lses[0], lses[1])


def _dilated_branch(zr, dil):
    n = WIN_STEPS
    b, _, rows, _ = zr.shape
    blk = (None, None, n, GROUP_W)
    cur = lambda slot: (lambda bi, r, nb: (bi, r, nb, slot))
    prev = lambda slot: (lambda bi, r, nb: (bi, r, jnp.maximum(nb - 1, 0), slot))
    return pl.pallas_call(
        _dilated_kernel,
        grid=(b, dil, rows // n),
        in_specs=[pl.BlockSpec(blk, cur(0)), pl.BlockSpec(blk, cur(1)), pl.BlockSpec(blk, prev(1)),
                  pl.BlockSpec(blk, cur(2)), pl.BlockSpec(blk, prev(2))],
        out_specs=[pl.BlockSpec(blk, cur(0)), pl.BlockSpec(blk, cur(0))],
        out_shape=[jax.ShapeDtypeStruct((b, dil, rows, GROUP_W), F32)] * 2,
        scratch_shapes=[pltpu.VMEM((N_HEADS, n, 2 * n), F32)],
        compiler_params=_cparams(3), name=f"dilated_attn_d{dil}",
    )(zr, zr, zr, zr, zr)


DSA_TQ = 256
P1_CHUNK = 256
P3_CHUNK = 256
CNT_ROWS = 32
BF16_ROWS = 16
VT_ROWS = HEAD_DIM + BF16_ROWS
BF16_MIN_NORMAL = 0x0080
CHECK_EVERY = 4


def _chunk_loop(n, body, init):
    carry = lax.fori_loop(0, n // 2, lambda i, c: body(2 * i + 1, body(2 * i, c)), init)
    return lax.cond(n % 2 == 1, lambda c: body(n - 1, c), lambda c: c, carry)


def _to_key(x):
    bits = pltpu.bitcast(x, jnp.int32)
    return bits ^ ((bits >> 31) & 0x7FFFFFFF)


def _dsa_kernel(qb_ref, qi_ref, wit_ref, kb_ref, vt_ref, ki_ref, ot_ref,
                key_ref, top_ref, qim_ref, qbm_ref, m_ref, acc_ref, thr_ref, s_ref, mx_ref,
                *, topk):
    assert P3_CHUNK == P1_CHUNK == DSA_TQ
    tq = DSA_TQ
    qblk = pl.program_id(1)
    t0 = qblk * tq
    n1 = qblk + 1
    masks = _head_lane_masks()

    for h in range(N_HEADS):
        j, hb = divmod(h, 2)
        sl = slice(j * LANES, (j + 1) * LANES)
        qi = qi_ref[:, sl]
        qb = qb_ref[:, sl]
        qim_ref[h] = jnp.where(masks[hb], qi, jnp.zeros_like(qi))
        qbm_ref[h] = jnp.where(masks[hb], qb, jnp.zeros_like(qb))

    def p1_chunk(c, diagonal):
        k0 = pl.multiple_of(c * P1_CHUNK, P1_CHUNK)
        kic = ki_ref[pl.ds(k0, P1_CHUNK), :]
        acc = jnp.zeros((P1_CHUNK, tq), F32)
        for h in range(N_HEADS):
            acc = acc + wit_ref[h:h + 1, :] * jnp.maximum(_dot_t(kic, qim_ref[h]), 0.0)
        acc = jnp.where(acc == 0.0, 0.0, acc)
        if diagonal:
            kpos = lax.broadcasted_iota(jnp.int32, (P1_CHUNK, tq), 0)
            qpos = lax.broadcasted_iota(jnp.int32, (P1_CHUNK, tq), 1)
            acc = jnp.where(kpos <= qpos, acc, -jnp.inf)
        key_ref[pl.ds(k0, P1_CHUNK), :] = _to_key(acc)
        top = pltpu.bitcast(pltpu.bitcast(acc, jnp.int32) & (-65536), F32)
        top_ref[pl.ds(k0, P1_CHUNK), :] = top.astype(BF16)

    def p1_pair(i, carry):
        p1_chunk(2 * i, False)
        p1_chunk(2 * i + 1, False)
        return carry

    lax.fori_loop(0, qblk // 2, p1_pair, 0)

    @pl.when(qblk % 2 == 1)
    def _():
        p1_chunk(qblk - 1, False)

    p1_chunk(qblk, True)

    def count_ge(thr):
        def body(c, cnt):
            k0 = pl.multiple_of(c * P1_CHUNK, P1_CHUNK)
            x = jnp.where(key_ref[pl.ds(k0, P1_CHUNK), :] >= thr, 1, 0)
            return cnt + jnp.sum(x.reshape(P1_CHUNK // CNT_ROWS, CNT_ROWS, tq), axis=0)

        cnt = _chunk_loop(n1, body, jnp.zeros((CNT_ROWS, tq), jnp.int32))
        return jnp.sum(cnt, axis=0, keepdims=True)

    def count_ge_top(mid16):
        pattern = mid16 ^ ((mid16 >> 15) & 0x7FFF)
        pattern = jnp.where((mid16 > 0) & (mid16 < BF16_MIN_NORMAL), BF16_MIN_NORMAL, pattern)
        thr_f = pltpu.bitcast(pattern << 16, F32)
        thr_b = jnp.broadcast_to(thr_f, (BF16_ROWS, tq)).astype(BF16)[None]
        one, zero = jnp.ones((), BF16), jnp.zeros((), BF16)

        def body(c, cnt):
            k0 = pl.multiple_of(c * P1_CHUNK, P1_CHUNK)
            x = top_ref[pl.ds(k0, P1_CHUNK), :].reshape(P1_CHUNK // BF16_ROWS, BF16_ROWS, tq)
            y = jnp.where(x >= thr_b, one, zero)
            parts = [y[r] for r in range(P1_CHUNK // BF16_ROWS)]
            while len(parts) > 1:
                parts = [a + b for a, b in zip(parts[::2], parts[1::2])]
            return cnt + parts[0].astype(F32)

        cnt = _chunk_loop(n1, body, jnp.zeros((BF16_ROWS, tq), F32))
        return jnp.sum(cnt, axis=0, keepdims=True).astype(jnp.int32)

    def step(st, count_fn, final):
        lo, hi, thr, done, cnt_hi = st
        mid = (lo >> 1) + (hi >> 1) + (((lo & 1) + (hi & 1) + 1) >> 1)
        c = count_fn(mid)
        ge = c >= topk
        active = done == 0
        hit = active & (c == topk)
        new_lo = jnp.where(active & ge, mid, lo)
        new_hi = jnp.where(active & jnp.logical_not(ge), mid - 1, hi)
        new_cnt_hi = jnp.where(active & jnp.logical_not(ge), c, cnt_hi)
        new_thr = jnp.where(hit, mid, thr)
        new_done = jnp.where(hit, 1, done)
        if final:
            conv = active & jnp.logical_not(hit) & (new_lo == new_hi)
            new_thr = jnp.where(conv, new_lo, new_thr)
            new_done = jnp.where(conv, 2, new_done)
        return new_lo, new_hi, new_thr, new_done, new_cnt_hi

    t_row = t0 + lax.broadcasted_iota(jnp.int32, (1, tq), 1)
    all_sel = t_row + 1 <= topk
    done0 = all_sel.astype(jnp.int32)
    zeros = jnp.zeros((1, tq), jnp.int32)
    lo16 = jnp.full((1, tq), (KEY_NEG_INF + 1) >> 16, jnp.int32)
    hi16 = jnp.full((1, tq), 0x7F7F, jnp.int32)
    st = lax.fori_loop(0, 16, lambda _, s: step(s, count_ge_top, False),
                       (lo16, hi16, zeros, done0, zeros))
    lo16, _, thr16, done, cnt_hi = st
    lo = lo16 << 16
    thr = jnp.where(all_sel, KEY_NEG_INF + 1, thr16 << 16)

    def fine_cond(carry):
        it, st = carry
        return (it < 16 // CHECK_EVERY) & (jnp.min(st[3]) == 0)

    def fine_body(carry):
        it, st = carry
        return it + 1, lax.fori_loop(0, CHECK_EVERY, lambda _, s: step(s, count_ge, True), st)

    _, st = lax.while_loop(fine_cond, fine_body, (0, (lo, lo + 0xFFFF, thr, done, cnt_hi)))
    _, _, thr, done, cnt_hi = st
    thr_ref[...] = thr

    tie = done == 2

    @pl.when(jnp.max(done) == 2)
    def _():
        need = topk - cnt_hi
        sub_i = lax.broadcasted_iota(jnp.int32, (P1_CHUNK, tq), 0)

        def count_eq_le(jmax):
            def body(c, cnt):
                k0 = pl.multiple_of(c * P1_CHUNK, P1_CHUNK)
                kc = key_ref[pl.ds(k0, P1_CHUNK), :]
                x = jnp.where((kc == thr) & (sub_i + k0 <= jmax), 1, 0)
                return cnt + jnp.sum(x.reshape(P1_CHUNK // CNT_ROWS, CNT_ROWS, tq), axis=0)

            cnt = lax.fori_loop(0, n1, body, jnp.zeros((CNT_ROWS, tq), jnp.int32))
            return jnp.sum(cnt, axis=0, keepdims=True)

        def jbody(_, st):
            jlo, jhi = st
            jmid = (jlo + jhi) >> 1
            ok = count_eq_le(jmid) >= need
            return jnp.where(ok, jlo, jmid + 1), jnp.where(ok, jmid, jhi)

        jlo, _ = lax.fori_loop(0, 14, jbody, (jnp.zeros((1, tq), jnp.int32),
                                               jnp.broadcast_to(t0 + tq - 1, (1, tq)).astype(jnp.int32)))
        jcut = jnp.where(tie, jlo, 0x7FFFFFFF)

        def demote(c, carry):
            k0 = pl.multiple_of(c * P1_CHUNK, P1_CHUNK)
            kc = key_ref[pl.ds(k0, P1_CHUNK), :]
            key_ref[pl.ds(k0, P1_CHUNK), :] = jnp.where((kc == thr) & (sub_i + k0 > jcut), kc - 1, kc)
            return carry

        lax.fori_loop(0, n1, demote, 0)

    m_ref[...] = jnp.full(m_ref.shape, MASK_VALUE, F32)
    acc_ref[...] = jnp.zeros(acc_ref.shape, F32)

    def scores(c, slot):
        k0 = pl.multiple_of(c * P3_CHUNK, P3_CHUNK)
        sel = key_ref[pl.ds(k0, P3_CHUNK), :] >= thr_ref[...]
        for h in range(N_HEADS):
            j = h // 2
            kc = kb_ref[pl.ds(k0, P3_CHUNK), j * LANES:(j + 1) * LANES]
            s = jnp.where(sel, _dot_t(kc, qbm_ref[h]), MASK_VALUE)
            s_ref[slot, h] = s
            mx_ref[slot, h] = jnp.max(s, axis=0, keepdims=True)

    def accumulate(c, slot):
        k0 = pl.multiple_of(c * P3_CHUNK, P3_CHUNK)
        for h in range(N_HEADS):
            rows = slice(h * VT_ROWS, (h + 1) * VT_ROWS)
            m_old = m_ref[h]
            m_new = jnp.maximum(m_old, mx_ref[slot, h])
            alpha = jnp.exp2(m_old - m_new)
            p = jnp.exp2((s_ref[slot, h] - m_new).astype(BF16))
            m_ref[h] = m_new
            pv = _dot(vt_ref[rows, pl.ds(k0, P3_CHUNK)], p)
            acc_ref[rows, :] = alpha * acc_ref[rows, :] + pv

    scores(0, 0)

    def p3_pair(i, carry):
        c = 2 * i
        scores(c + 1, 1)
        accumulate(c, 0)
        scores(c + 2, 0)
        accumulate(c + 1, 1)
        return carry

    lax.fori_loop(0, (n1 - 1) // 2, p3_pair, 0)

    @pl.when(n1 % 2 == 1)
    def _():
        accumulate(n1 - 1, 0)

    @pl.when(n1 % 2 == 0)
    def _():
        scores(n1 - 1, 1)
        accumulate(n1 - 2, 0)
        accumulate(n1 - 1, 1)

    for h in range(N_HEADS):
        base = h * VT_ROWS
        denom = acc_ref[base + HEAD_DIM:base + HEAD_DIM + 1, :]
        ot_ref[h * HEAD_DIM:(h + 1) * HEAD_DIM, :] = (acc_ref[base:base + HEAD_DIM, :] / denom).astype(BF16)


def _sparse_attention(zb, zi, wi, b, t, topk):
    tq = DSA_TQ
    zb3 = zb.reshape(b, t, 3 * GROUP_W)
    zi3 = zi.reshape(b, t, GROUP_W + LANES)
    vt = zb3[:, :, 2 * GROUP_W:].transpose(0, 2, 1).reshape(b, N_HEADS, HEAD_DIM, t)
    pad = jnp.zeros((b, N_HEADS, VT_ROWS - HEAD_DIM, t), BF16).at[:, :, 0, :].set(1.0)
    vt = jnp.concatenate([vt, pad], axis=2).reshape(b, N_HEADS * VT_ROWS, t)
    wit = wi.reshape(b, t, LANES)[:, :, :N_HEADS].transpose(0, 2, 1)
    qmap = lambda bi, qi: (bi, qi, 0)
    tmap = lambda bi, qi: (bi, 0, qi)
    out_t = pl.pallas_call(
        functools.partial(_dsa_kernel, topk=topk),
        grid=(b, t // tq),
        in_specs=[pl.BlockSpec((None, tq, GROUP_W), qmap),
                  pl.BlockSpec((None, tq, GROUP_W), qmap),
                  pl.BlockSpec((None, N_HEADS, tq), tmap),
                  pl.BlockSpec((None, t, GROUP_W), lambda bi, qi: (bi, 0, 1),
                               pipeline_mode=pl.Buffered(1)),
                  pl.BlockSpec((None, N_HEADS * VT_ROWS, t), lambda bi, qi: (bi, 0, 0),
                               pipeline_mode=pl.Buffered(1)),
                  pl.BlockSpec((None, t, LANES), lambda bi, qi: (bi, 0, GROUP_W // LANES),
                               pipeline_mode=pl.Buffered(1))],
        out_specs=pl.BlockSpec((None, GROUP_W, tq), tmap),
        out_shape=jax.ShapeDtypeStruct((b, GROUP_W, t), BF16),
        scratch_shapes=[pltpu.VMEM((t, tq), jnp.int32),
                        pltpu.VMEM((t, tq), BF16),
                        pltpu.VMEM((N_HEADS, tq, LANES), BF16),
                        pltpu.VMEM((N_HEADS, tq, LANES), BF16),
                        pltpu.VMEM((N_HEADS, 1, tq), F32),
                        pltpu.VMEM((N_HEADS * VT_ROWS, tq), F32),
                        pltpu.VMEM((1, tq), jnp.int32),
                        pltpu.VMEM((2, N_HEADS, P3_CHUNK, tq), F32),
                        pltpu.VMEM((2, N_HEADS, 1, tq), F32)],
        compiler_params=_cparams(2), name="dsa_attn",
    )(zb3, zi3, wit, zb3, vt, zi3)
    return out_t.transpose(0, 2, 1).reshape(b * t, GROUP_W)


def _to_token_order(r_ref, pt_ref):
    r = r_ref[...].reshape(PERM_ROWS, GROUP_W)
    hi = r.astype(BF16)
    lo = (r - hi.astype(F32)).astype(BF16)
    return _dot(pt_ref[...], hi) + _dot(pt_ref[...], lo)


def _out_proj_kernel(x_ref, o1_ref, l1_ref, o4_ref, l4_ref, o16_ref, l16_ref, ob_ref, wo_ref,
                     pt4_ref, pt16_ref, y_ref):
    o1, l1 = o1_ref[...], l1_ref[...]
    o2, l2 = _to_token_order(o4_ref, pt4_ref), _to_token_order(l4_ref, pt4_ref)
    o3, l3 = _to_token_order(o16_ref, pt16_ref), _to_token_order(l16_ref, pt16_ref)
    m = jnp.maximum(jnp.maximum(l1, l2), l3)
    e1, e2, e3 = jnp.exp(l1 - m), jnp.exp(l2 - m), jnp.exp(l3 - m)
    out_a = (e1 * o1 + e2 * o2 + e3 * o3) / (e1 + e2 + e3)
    y = x_ref[...] + _dot(out_a.astype(BF16), wo_ref[0:GROUP_W, :])
    y_ref[...] = y + _dot(ob_ref[...], wo_ref[GROUP_W:2 * GROUP_W, :])


def _out_proj(x2, branches, ob, w_o, t):
    n, d = x2.shape
    tm = PERM_ROWS
    tiles = t // tm
    row = lambda i: (i, 0)
    const = lambda i: (0, 0)
    res_map = lambda i: (i // tiles, 0, i % tiles, 0)
    gspec = pl.BlockSpec((tm, GROUP_W), row)
    rspec = lambda dil: pl.BlockSpec((None, dil, tm // dil, GROUP_W), res_map)
    pspec = pl.BlockSpec((tm, tm), const)
    (o1, l1), (o4, l4), (o16, l16) = branches
    return pl.pallas_call(
        _out_proj_kernel,
        grid=(n // tm,),
        in_specs=[pl.BlockSpec((tm, d), row), gspec, gspec, rspec(4), rspec(4), rspec(16), rspec(16),
                  gspec, pl.BlockSpec((2 * GROUP_W, d), const), pspec, pspec],
        out_specs=pl.BlockSpec((tm, d), row),
        out_shape=jax.ShapeDtypeStruct((n, d), F32),
        compiler_params=_cparams(1), name="merge_out_proj",
    )(x2, o1.reshape(n, GROUP_W), l1.reshape(n, GROUP_W), o4, l4, o16, l16, ob, w_o.astype(BF16),
      jnp.asarray(_residue_perm(4).T, BF16), jnp.asarray(_residue_perm(16).T, BF16))


HALO = 8


def _ffn_kernel(x_ref, g_ref, wup_ref, cw_ref, cb_ref, wdn_ref, p_ref, gp_ref, wg_ref, wp_ref, gf_ref,
                y_ref, acc_ref, buf_ref, tail_ref, *, tiles_per_seq, final):
    tm = x_ref.shape[0]
    i = pl.program_id(0)

    @pl.when(i % tiles_per_seq == 0)
    def _():
        tail_ref[...] = jnp.zeros(tail_ref.shape, F32)

    x = x_ref[...]
    h = _rms(x, g_ref[...]).astype(BF16)
    acc_ref[...] = x

    def conv(c):
        u = _dot(h, wup_ref[c])
        buf_ref[0:HALO, :] = tail_ref[c]
        buf_ref[HALO:HALO + tm, :] = u
        tail_ref[c] = u[tm - HALO:, :]
        w = cw_ref[c]
        y = cb_ref[c] + w[2:3, :] * u
        y = y + w[1:2, :] * buf_ref[HALO - 1:HALO - 1 + tm, :]
        return y + w[0:1, :] * buf_ref[HALO - 2:HALO - 2 + tm, :]

    def body(c, carry):
        gate = conv(c)
        up = conv(c + N_FF_CHUNKS)
        act = (gate * jax.nn.sigmoid(gate) * up).astype(BF16)
        acc_ref[...] += _dot(act, wdn_ref[c])
        return carry

    lax.fori_loop(0, N_FF_CHUNKS, body, 0, unroll=True)
    x = acc_ref[...]
    gate = jax.nn.sigmoid(_dot(_rms(x, gp_ref[...]).astype(BF16), wg_ref[...]))
    x = x + gate * _dot(p_ref[...].astype(BF16), wp_ref[...])
    y_ref[...] = _rms(x, gf_ref[...]) if final else x


def _ffn_ple(x2, t, ffn_norm, w_up, conv_w, conv_b, w_down, p2, ple_norm, w_gate, w_proj,
             final_norm, final, tm):
    n, d = x2.shape
    pd = p2.shape[1]
    nc2 = 2 * N_FF_CHUNKS
    wup = w_up.reshape(d, nc2, FF_CHUNK).transpose(1, 0, 2).astype(BF16)
    cw = conv_w.reshape(CONV_WIDTH, nc2, FF_CHUNK).transpose(1, 0, 2)
    cb = conv_b.reshape(nc2, 1, FF_CHUNK)
    wdn = w_down.reshape(N_FF_CHUNKS, FF_CHUNK, d).astype(BF16)
    row = lambda i: (i, 0)
    const = lambda i: (0, 0)
    c3 = lambda i: (0, 0, 0)
    once = pl.Buffered(1)
    return pl.pallas_call(
        functools.partial(_ffn_kernel, tiles_per_seq=t // tm, final=final),
        grid=(n // tm,),
        in_specs=[pl.BlockSpec((tm, d), row), pl.BlockSpec((1, d), const),
                  pl.BlockSpec((nc2, d, FF_CHUNK), c3, pipeline_mode=once),
                  pl.BlockSpec((nc2, CONV_WIDTH, FF_CHUNK), c3),
                  pl.BlockSpec((nc2, 1, FF_CHUNK), c3),
                  pl.BlockSpec((N_FF_CHUNKS, FF_CHUNK, d), c3, pipeline_mode=once),
                  pl.BlockSpec((tm, pd), row), pl.BlockSpec((1, d), const),
                  pl.BlockSpec((d, d), const, pipeline_mode=once),
                  pl.BlockSpec((pd, d), const, pipeline_mode=once),
                  pl.BlockSpec((1, d), const)],
        out_specs=pl.BlockSpec((tm, d), row),
        out_shape=jax.ShapeDtypeStruct((n, d), F32),
        scratch_shapes=[pltpu.VMEM((tm, d), F32), pltpu.VMEM((tm + HALO, FF_CHUNK), F32),
                        pltpu.VMEM((nc2, HALO, FF_CHUNK), F32)],
        compiler_params=_cparams(1), name="conv_ffn_ple",
    )(x2, ffn_norm.reshape(1, d), wup, cw, cb, wdn, p2, ple_norm.reshape(1, d),
      w_gate.astype(BF16), w_proj.astype(BF16), final_norm.reshape(1, d))


PROJ_TM = 512


def kernel(x, p, positions, attn_norm, w_in, w_o, ffn_norm, w_up, conv_w, conv_b, w_down,
           ple_norm, w_ple_gate, w_ple_proj, final_norm):
    b, t, d = x.shape
    depth = p.shape[0]
    n = b * t
    topk = min(TOPK_MAX, t // 4)
    assert t % DILATED_PAIRS[-1][0] == 0 and t % DSA_TQ == 0 and t % PROJ_TM == 0
    x2 = x.reshape(n, d)
    pos2 = positions.reshape(n, 1)
    for i in range(depth):
        za, zb, zi, wi, za4, za16 = _project(x2, pos2, attn_norm[i], w_in[i], b, t, PROJ_TM)
        views = (za.reshape(b, 1, t, 3 * GROUP_W), za4, za16)
        branches = [_dilated_branch(zr, dil) for zr, (_, dil) in zip(views, DILATED_PAIRS)]
        ob = _sparse_attention(zb, zi, wi, b, t, topk)
        x2 = _out_proj(x2, branches, ob, w_o[i], t)
        x2 = _ffn_ple(x2, t, ffn_norm[i], w_up[i], conv_w[i], conv_b[i], w_down[i],
                      p[i].reshape(n, -1), ple_norm[i], w_ple_gate[i], w_ple_proj[i],
                      final_norm, i == depth - 1, PROJ_TM)
    return x2.reshape(b, t, d)
```

```python
import functools
import math

import numpy as np
import jax
import jax.numpy as jnp
from jax import lax
from jax.experimental import pallas as pl
from jax.experimental.pallas import tpu as pltpu

F32 = jnp.float32
BF16 = jnp.bfloat16

HEAD_DIM = 64
HALF = HEAD_DIM // 2
N_HEADS = 8
GROUP_W = N_HEADS * HEAD_DIM
N_PAIRS = N_HEADS // 2
LANES = 128
DILATED_PAIRS = ((128, 1), (512, 4), (2048, 16))
WIN_STEPS = 128
TOPK_MAX = 256
D_FF = 2816
FF_CHUNK = 256
N_FF_CHUNKS = D_FF // FF_CHUNK
CONV_WIDTH = 3
ROPE_THETA = 10000.0
RMS_EPS = 1e-6
MASK_VALUE = -1e30
KEY_NEG_INF = -2139095041
VMEM_LIMIT = 56 * 1024 * 1024


def _cparams(n_axes):
    return pltpu.CompilerParams(dimension_semantics=("arbitrary",) * n_axes,
                                vmem_limit_bytes=VMEM_LIMIT)


def _dot_t(a, b):
    return lax.dot_general(a, b, (((1,), (1,)), ((), ())), preferred_element_type=F32)


def _dot(a, b):
    return jnp.dot(a, b, preferred_element_type=F32)


def _rms(x, g):
    ms = jnp.mean(x * x, axis=-1, keepdims=True)
    return (x * lax.rsqrt(ms + RMS_EPS)) * g


def _pair_perm():
    idx = []
    for j in range(N_PAIRS):
        a, b = 2 * j * HEAD_DIM, (2 * j + 1) * HEAD_DIM
        idx += list(range(a, a + HALF)) + list(range(b, b + HALF))
        idx += list(range(a + HALF, a + HEAD_DIM)) + list(range(b + HALF, b + HEAD_DIM))
    return np.asarray(idx, np.int32)


def _in_proj_columns():
    perm = _pair_perm()
    nat = np.arange(GROUP_W, dtype=np.int32)
    cols = []
    for g, roped in enumerate((True, True, False, True, True, False, True)):
        cols.append(g * GROUP_W + (perm if roped else nat))
    kbase = 7 * GROUP_W
    k1 = kbase + np.arange(HALF, dtype=np.int32)
    k2 = kbase + HALF + np.arange(HALF, dtype=np.int32)
    cols.append(np.concatenate([k1, k1, k2, k2]))
    return np.concatenate(cols)


PERM_ROWS = 256


def _residue_perm(dil):
    p = np.zeros((PERM_ROWS, PERM_ROWS), np.float32)
    per = PERM_ROWS // dil
    for r in range(dil):
        for u in range(per):
            p[r * per + u, u * dil + r] = 1.0
    return p


def _proj_kernel(x_ref, pos_ref, g_ref, invf_ref, w_ref, wiw_ref, p4_ref, p16_ref,
                 za_ref, zb_ref, zi_ref, wi_ref, za4_ref, za16_ref):
    x = x_ref[...]
    h32 = _rms(x, g_ref[...])
    h = h32.astype(BF16)
    ang = pos_ref[...].astype(F32) * invf_ref[...]
    lane = lax.broadcasted_iota(jnp.int32, (1, LANES), 1)
    cos = jnp.cos(ang)
    sin = jnp.sin(ang) * jnp.where(lane < 2 * HALF, -1.0, 1.0)

    def rope(z):
        return z * cos + pltpu.roll(z, 2 * HALF, 1) * sin

    q_scale = HEAD_DIM ** -0.5
    plan = ((za_ref, 0, True, q_scale), (za_ref, 1, True, 1.0), (za_ref, 2, False, 1.0),
            (zb_ref, 0, True, q_scale * math.log2(math.e)), (zb_ref, 1, True, 1.0), (zb_ref, 2, False, 1.0),
            (zi_ref, 0, True, q_scale))
    for g, (out_ref, slot, roped, scale) in enumerate(plan):
        z = _dot(h, w_ref[:, g * GROUP_W:(g + 1) * GROUP_W])
        for j in range(N_PAIRS):
            zj = z[:, j * LANES:(j + 1) * LANES]
            if roped:
                zj = rope(zj)
            if scale != 1.0:
                zj = zj * scale
            out_ref[:, slot * GROUP_W + j * LANES: slot * GROUP_W + (j + 1) * LANES] = zj.astype(BF16)
    zk = _dot(h, w_ref[:, 7 * GROUP_W: 7 * GROUP_W + LANES])
    zi_ref[:, GROUP_W:GROUP_W + LANES] = rope(zk).astype(BF16)
    wi_ref[...] = _dot(h, wiw_ref[...]) * (N_HEADS ** -0.5)
    for half in range(x.shape[0] // PERM_ROWS):
        za = za_ref[half * PERM_ROWS:(half + 1) * PERM_ROWS, :]
        for dil, p_ref, out_ref in ((4, p4_ref, za4_ref), (16, p16_ref, za16_ref)):
            per = PERM_ROWS // dil
            zp = _dot(p_ref[...], za).astype(BF16)
            for r in range(dil):
                out_ref[r, half * per:(half + 1) * per, :] = zp[r * per:(r + 1) * per, :]


def _project(x2, pos2, attn_norm, w_in, b, t, tm):
    n, d = x2.shape
    tiles = t // tm
    res_map = lambda i: (i // tiles, 0, i % tiles, 0)
    cols = _in_proj_columns()
    w_main = jnp.take(w_in, cols, axis=1).astype(BF16)
    w_wi = jnp.pad(w_in[:, 7 * GROUP_W + HEAD_DIM:], ((0, 0), (0, LANES - N_HEADS))).astype(BF16)
    inv_freq = 1.0 / (ROPE_THETA ** (np.arange(0, HEAD_DIM, 2, dtype=np.float32) / HEAD_DIM))
    invf = jnp.asarray(np.tile(inv_freq, 4)[None, :], F32)
    wcols = w_main.shape[1]
    const = lambda i: (0, 0)
    row = lambda i: (i, 0)
    return pl.pallas_call(
        _proj_kernel,
        grid=(n // tm,),
        in_specs=[pl.BlockSpec((tm, d), row), pl.BlockSpec((tm, 1), row),
                  pl.BlockSpec((1, d), const), pl.BlockSpec((1, LANES), const),
                  pl.BlockSpec((d, wcols), const, pipeline_mode=pl.Buffered(1)),
                  pl.BlockSpec((d, LANES), const),
                  pl.BlockSpec((PERM_ROWS, PERM_ROWS), const),
                  pl.BlockSpec((PERM_ROWS, PERM_ROWS), const)],
        out_specs=[pl.BlockSpec((tm, 3 * GROUP_W), row), pl.BlockSpec((tm, 3 * GROUP_W), row),
                   pl.BlockSpec((tm, GROUP_W + LANES), row), pl.BlockSpec((tm, LANES), row),
                   pl.BlockSpec((None, 4, tm // 4, 3 * GROUP_W), res_map),
                   pl.BlockSpec((None, 16, tm // 16, 3 * GROUP_W), res_map)],
        out_shape=[jax.ShapeDtypeStruct((n, 3 * GROUP_W), BF16),
                   jax.ShapeDtypeStruct((n, 3 * GROUP_W), BF16),
                   jax.ShapeDtypeStruct((n, GROUP_W + LANES), BF16),
                   jax.ShapeDtypeStruct((n, LANES), F32),
                   jax.ShapeDtypeStruct((b, 4, t // 4, 3 * GROUP_W), BF16),
                   jax.ShapeDtypeStruct((b, 16, t // 16, 3 * GROUP_W), BF16)],
        compiler_params=_cparams(1), name="in_proj_rope",
    )(x2, pos2, attn_norm.reshape(1, d), invf, w_main, w_wi,
      jnp.asarray(_residue_perm(4), BF16), jnp.asarray(_residue_perm(16), BF16))


def _head_lane_masks():
    lane = lax.broadcasted_iota(jnp.int32, (1, LANES), 1)
    first = (lane % HEAD_DIM) < HALF
    return first, jnp.logical_not(first)


def _dilated_kernel(q_ref, kc_ref, kp_ref, vc_ref, vp_ref, o_ref, lse_ref, s_ref):
    nb = pl.program_id(2)
    n = WIN_STEPS
    a = lax.broadcasted_iota(jnp.int32, (n, 2 * n), 0)
    c = lax.broadcasted_iota(jnp.int32, (n, 2 * n), 1)
    band = (c >= a) & (c <= a + n) & ((nb > 0) | (c >= n))
    masks = _head_lane_masks()
    lane = lax.broadcasted_iota(jnp.int32, (1, LANES), 1)
    low = lane < HEAD_DIM
    for h in range(N_HEADS):
        j, hb = divmod(h, 2)
        sl = slice(j * LANES, (j + 1) * LANES)
        q = q_ref[:, sl]
        kk = jnp.concatenate([kp_ref[:, sl], kc_ref[:, sl]], axis=0)
        qm = jnp.where(masks[hb], q, jnp.zeros_like(q))
        s_ref[h] = jnp.where(band, _dot_t(qm, kk), -jnp.inf)
    lse_tile = jnp.zeros((n, LANES), F32)
    for j in range(N_PAIRS):
        sl = slice(j * LANES, (j + 1) * LANES)
        vv = jnp.concatenate([vp_ref[:, sl], vc_ref[:, sl]], axis=0)
        outs = []
        for hb in range(2):
            s = s_ref[2 * j + hb]
            m = jnp.max(s, axis=1, keepdims=True)
            p = jnp.exp(s - m)
            l = jnp.sum(p, axis=1, keepdims=True)
            outs.append(_dot(p.astype(BF16), vv) / l)
            lse_tile = jnp.where(lane == 2 * j + hb, m + jnp.log(l), lse_tile)
        o_ref[:, sl] = jnp.where(low, outs[0], outs[1]).astype(BF16)
    lse_ref[...] = lse_tile


def _dilated_branch(zr, dil):
    n = WIN_STEPS
    b, _, rows, _ = zr.shape
    blk = (None, None, n, GROUP_W)
    cur = lambda slot: (lambda bi, r, nb: (bi, r, nb, slot))
    prev = lambda slot: (lambda bi, r, nb: (bi, r, jnp.maximum(nb - 1, 0), slot))
    return pl.pallas_call(
        _dilated_kernel,
        grid=(b, dil, rows // n),
        in_specs=[pl.BlockSpec(blk, cur(0)), pl.BlockSpec(blk, cur(1)), pl.BlockSpec(blk, prev(1)),
                  pl.BlockSpec(blk, cur(2)), pl.BlockSpec(blk, prev(2))],
        out_specs=[pl.BlockSpec(blk, cur(0)), pl.BlockSpec((None, None, n, LANES), cur(0))],
        out_shape=[jax.ShapeDtypeStruct((b, dil, rows, GROUP_W), BF16),
                   jax.ShapeDtypeStruct((b, dil, rows, LANES), F32)],
        scratch_shapes=[pltpu.VMEM((N_HEADS, n, 2 * n), F32)],
        compiler_params=_cparams(3), name=f"dilated_attn_d{dil}",
    )(zr, zr, zr, zr, zr)


DSA_TQ = 256
P1_CHUNK = 256
P3_CHUNK = 256
CNT_ROWS = 32
BF16_ROWS = 16
VT_ROWS = HEAD_DIM + BF16_ROWS
BF16_MIN_NORMAL = 0x0080
CHECK_EVERY = 4


def _chunk_loop(n, body, init):
    carry = lax.fori_loop(0, n // 2, lambda i, c: body(2 * i + 1, body(2 * i, c)), init)
    return lax.cond(n % 2 == 1, lambda c: body(n - 1, c), lambda c: c, carry)


def _to_key(x):
    bits = pltpu.bitcast(x, jnp.int32)
    return bits ^ ((bits >> 31) & 0x7FFFFFFF)


def _dsa_kernel(qb_ref, qi_ref, wit_ref, kb_ref, vt_ref, ki_ref, ot_ref,
                key_ref, top_ref, qim_ref, qbm_ref, m_ref, acc_ref, thr_ref, s_ref, mx_ref,
                *, topk):
    assert P3_CHUNK == P1_CHUNK == DSA_TQ
    tq = DSA_TQ
    qblk = pl.program_id(1)
    t0 = qblk * tq
    n1 = qblk + 1
    masks = _head_lane_masks()

    for h in range(N_HEADS):
        j, hb = divmod(h, 2)
        sl = slice(j * LANES, (j + 1) * LANES)
        qi = qi_ref[:, sl]
        qb = qb_ref[:, sl]
        qim_ref[h] = jnp.where(masks[hb], qi, jnp.zeros_like(qi))
        qbm_ref[h] = jnp.where(masks[hb], qb, jnp.zeros_like(qb))

    def p1_chunk(c, diagonal):
        k0 = pl.multiple_of(c * P1_CHUNK, P1_CHUNK)
        kic = ki_ref[pl.ds(k0, P1_CHUNK), :]
        acc = jnp.zeros((P1_CHUNK, tq), F32)
        for h in range(N_HEADS):
            acc = acc + wit_ref[h:h + 1, :] * jnp.maximum(_dot_t(kic, qim_ref[h]), 0.0)
        acc = jnp.where(acc == 0.0, 0.0, acc)
        if diagonal:
            kpos = lax.broadcasted_iota(jnp.int32, (P1_CHUNK, tq), 0)
            qpos = lax.broadcasted_iota(jnp.int32, (P1_CHUNK, tq), 1)
            acc = jnp.where(kpos <= qpos, acc, -jnp.inf)
        key_ref[pl.ds(k0, P1_CHUNK), :] = _to_key(acc)
        top = pltpu.bitcast(pltpu.bitcast(acc, jnp.int32) & (-65536), F32)
        top_ref[pl.ds(k0, P1_CHUNK), :] = top.astype(BF16)

    def p1_pair(i, carry):
        p1_chunk(2 * i, False)
        p1_chunk(2 * i + 1, False)
        return carry

    lax.fori_loop(0, qblk // 2, p1_pair, 0)

    @pl.when(qblk % 2 == 1)
    def _():
        p1_chunk(qblk - 1, False)

    p1_chunk(qblk, True)

    def count_ge(thr):
        def body(c, cnt):
            k0 = pl.multiple_of(c * P1_CHUNK, P1_CHUNK)
            x = jnp.where(key_ref[pl.ds(k0, P1_CHUNK), :] >= thr, 1, 0)
            return cnt + jnp.sum(x.reshape(P1_CHUNK // CNT_ROWS, CNT_ROWS, tq), axis=0)

        cnt = _chunk_loop(n1, body, jnp.zeros((CNT_ROWS, tq), jnp.int32))
        return jnp.sum(cnt, axis=0, keepdims=True)

    def count_ge_top(mid16):
        pattern = mid16 ^ ((mid16 >> 15) & 0x7FFF)
        pattern = jnp.where((mid16 > 0) & (mid16 < BF16_MIN_NORMAL), BF16_MIN_NORMAL, pattern)
        thr_f = pltpu.bitcast(pattern << 16, F32)
        thr_b = jnp.broadcast_to(thr_f, (BF16_ROWS, tq)).astype(BF16)[None]
        one, zero = jnp.ones((), BF16), jnp.zeros((), BF16)

        def body(c, cnt):
            k0 = pl.multiple_of(c * P1_CHUNK, P1_CHUNK)
            x = top_ref[pl.ds(k0, P1_CHUNK), :].reshape(P1_CHUNK // BF16_ROWS, BF16_ROWS, tq)
            y = jnp.where(x >= thr_b, one, zero)
            parts = [y[r] for r in range(P1_CHUNK // BF16_ROWS)]
            while len(parts) > 1:
                parts = [a + b for a, b in zip(parts[::2], parts[1::2])]
            return cnt + parts[0].astype(F32)

        cnt = _chunk_loop(n1, body, jnp.zeros((BF16_ROWS, tq), F32))
        return jnp.sum(cnt, axis=0, keepdims=True).astype(jnp.int32)

    def step(st, count_fn, final):
        lo, hi, thr, done, cnt_hi = st
        mid = (lo >> 1) + (hi >> 1) + (((lo & 1) + (hi & 1) + 1) >> 1)
        c = count_fn(mid)
        ge = c >= topk
        active = done == 0
        hit = active & (c == topk)
        new_lo = jnp.where(active & ge, mid, lo)
        new_hi = jnp.where(active & jnp.logical_not(ge), mid - 1, hi)
        new_cnt_hi = jnp.where(active & jnp.logical_not(ge), c, cnt_hi)
        new_thr = jnp.where(hit, mid, thr)
        new_done = jnp.where(hit, 1, done)
        if final:
            conv = active & jnp.logical_not(hit) & (new_lo == new_hi)
            new_thr = jnp.where(conv, new_lo, new_thr)
            new_done = jnp.where(conv, 2, new_done)
        return new_lo, new_hi, new_thr, new_done, new_cnt_hi

    t_row = t0 + lax.broadcasted_iota(jnp.int32, (1, tq), 1)
    all_sel = t_row + 1 <= topk
    done0 = all_sel.astype(jnp.int32)
    zeros = jnp.zeros((1, tq), jnp.int32)
    lo16 = jnp.full((1, tq), (KEY_NEG_INF + 1) >> 16, jnp.int32)
    hi16 = jnp.full((1, tq), 0x7F7F, jnp.int32)
    st = lax.fori_loop(0, 16, lambda _, s: step(s, count_ge_top, False),
                       (lo16, hi16, zeros, done0, zeros))
    lo16, _, thr16, done, cnt_hi = st
    lo = lo16 << 16
    thr = jnp.where(all_sel, KEY_NEG_INF + 1, thr16 << 16)

    def fine_cond(carry):
        it, st = carry
        return (it < 16 // CHECK_EVERY) & (jnp.min(st[3]) == 0)

    def fine_body(carry):
        it, st = carry
        return it + 1, lax.fori_loop(0, CHECK_EVERY, lambda _, s: step(s, count_ge, True), st)

    _, st = lax.while_loop(fine_cond, fine_body, (0, (lo, lo + 0xFFFF, thr, done, cnt_hi)))
    _, _, thr, done, cnt_hi = st
    thr_ref[...] = thr

    tie = done == 2

    @pl.when(jnp.max(done) == 2)
    def _():
        need = topk - cnt_hi
        sub_i = lax.broadcasted_iota(jnp.int32, (P1_CHUNK, tq), 0)

        def count_eq_le(jmax):
            def body(c, cnt):
                k0 = pl.multiple_of(c * P1_CHUNK, P1_CHUNK)
                kc = key_ref[pl.ds(k0, P1_CHUNK), :]
                x = jnp.where((kc == thr) & (sub_i + k0 <= jmax), 1, 0)
                return cnt + jnp.sum(x.reshape(P1_CHUNK // CNT_ROWS, CNT_ROWS, tq), axis=0)

            cnt = lax.fori_loop(0, n1, body, jnp.zeros((CNT_ROWS, tq), jnp.int32))
            return jnp.sum(cnt, axis=0, keepdims=True)

        def jbody(_, st):
            jlo, jhi = st
            jmid = (jlo + jhi) >> 1
            ok = count_eq_le(jmid) >= need
            return jnp.where(ok, jlo, jmid + 1), jnp.where(ok, jmid, jhi)

        jlo, _ = lax.fori_loop(0, 14, jbody, (jnp.zeros((1, tq), jnp.int32),
                                               jnp.broadcast_to(t0 + tq - 1, (1, tq)).astype(jnp.int32)))
        jcut = jnp.where(tie, jlo, 0x7FFFFFFF)

        def demote(c, carry):
            k0 = pl.multiple_of(c * P1_CHUNK, P1_CHUNK)
            kc = key_ref[pl.ds(k0, P1_CHUNK), :]
            key_ref[pl.ds(k0, P1_CHUNK), :] = jnp.where((kc == thr) & (sub_i + k0 > jcut), kc - 1, kc)
            return carry

        lax.fori_loop(0, n1, demote, 0)

    m_ref[...] = jnp.full(m_ref.shape, MASK_VALUE, F32)
    acc_ref[...] = jnp.zeros(acc_ref.shape, F32)

    def scores(c, slot):
        k0 = pl.multiple_of(c * P3_CHUNK, P3_CHUNK)
        sel = key_ref[pl.ds(k0, P3_CHUNK), :] >= thr_ref[...]
        for h in range(N_HEADS):
            j = h // 2
            kc = kb_ref[pl.ds(k0, P3_CHUNK), j * LANES:(j + 1) * LANES]
            s = jnp.where(sel, _dot_t(kc, qbm_ref[h]), MASK_VALUE)
            s_ref[slot, h] = s
            mx_ref[slot, h] = jnp.max(s, axis=0, keepdims=True)

    def accumulate(c, slot):
        k0 = pl.multiple_of(c * P3_CHUNK, P3_CHUNK)
        for h in range(N_HEADS):
            rows = slice(h * VT_ROWS, (h + 1) * VT_ROWS)
            m_old = m_ref[h]
            m_new = jnp.maximum(m_old, mx_ref[slot, h])
            alpha = jnp.exp2(m_old - m_new)
            p = jnp.exp2((s_ref[slot, h] - m_new).astype(BF16))
            m_ref[h] = m_new
            pv = _dot(vt_ref[rows, pl.ds(k0, P3_CHUNK)], p)
            acc_ref[rows, :] = alpha * acc_ref[rows, :] + pv

    scores(0, 0)

    def p3_pair(i, carry):
        c = 2 * i
        scores(c + 1, 1)
        accumulate(c, 0)
        scores(c + 2, 0)
        accumulate(c + 1, 1)
        return carry

    lax.fori_loop(0, (n1 - 1) // 2, p3_pair, 0)

    @pl.when(n1 % 2 == 1)
    def _():
        accumulate(n1 - 1, 0)

    @pl.when(n1 % 2 == 0)
    def _():
        scores(n1 - 1, 1)
        accumulate(n1 - 2, 0)
        accumulate(n1 - 1, 1)

    for h in range(N_HEADS):
        base = h * VT_ROWS
        denom = acc_ref[base + HEAD_DIM:base + HEAD_DIM + 1, :]
        ot_ref[h * HEAD_DIM:(h + 1) * HEAD_DIM, :] = (acc_ref[base:base + HEAD_DIM, :] / denom).astype(BF16)


def _sparse_attention(zb, zi, wi, b, t, topk):
    tq = DSA_TQ
    zb3 = zb.reshape(b, t, 3 * GROUP_W)
    zi3 = zi.reshape(b, t, GROUP_W + LANES)
    vt = zb3[:, :, 2 * GROUP_W:].transpose(0, 2, 1).reshape(b, N_HEADS, HEAD_DIM, t)
    pad = jnp.zeros((b, N_HEADS, VT_ROWS - HEAD_DIM, t), BF16).at[:, :, 0, :].set(1.0)
    vt = jnp.concatenate([vt, pad], axis=2).reshape(b, N_HEADS * VT_ROWS, t)
    wit = wi.reshape(b, t, LANES)[:, :, :N_HEADS].transpose(0, 2, 1)
    qmap = lambda bi, qi: (bi, qi, 0)
    tmap = lambda bi, qi: (bi, 0, qi)
    out_t = pl.pallas_call(
        functools.partial(_dsa_kernel, topk=topk),
        grid=(b, t // tq),
        in_specs=[pl.BlockSpec((None, tq, GROUP_W), qmap),
                  pl.BlockSpec((None, tq, GROUP_W), qmap),
                  pl.BlockSpec((None, N_HEADS, tq), tmap),
                  pl.BlockSpec((None, t, GROUP_W), lambda bi, qi: (bi, 0, 1),
                               pipeline_mode=pl.Buffered(1)),
                  pl.BlockSpec((None, N_HEADS * VT_ROWS, t), lambda bi, qi: (bi, 0, 0),
                               pipeline_mode=pl.Buffered(1)),
                  pl.BlockSpec((None, t, LANES), lambda bi, qi: (bi, 0, GROUP_W // LANES),
                               pipeline_mode=pl.Buffered(1))],
        out_specs=pl.BlockSpec((None, GROUP_W, tq), tmap),
        out_shape=jax.ShapeDtypeStruct((b, GROUP_W, t), BF16),
        scratch_shapes=[pltpu.VMEM((t, tq), jnp.int32),
                        pltpu.VMEM((t, tq), BF16),
                        pltpu.VMEM((N_HEADS, tq, LANES), BF16),
                        pltpu.VMEM((N_HEADS, tq, LANES), BF16),
                        pltpu.VMEM((N_HEADS, 1, tq), F32),
                        pltpu.VMEM((N_HEADS * VT_ROWS, tq), F32),
                        pltpu.VMEM((1, tq), jnp.int32),
                        pltpu.VMEM((2, N_HEADS, P3_CHUNK, tq), F32),
                        pltpu.VMEM((2, N_HEADS, 1, tq), F32)],
        compiler_params=_cparams(2), name="dsa_attn",
    )(zb3, zi3, wit, zb3, vt, zi3)
    return out_t.transpose(0, 2, 1).reshape(b * t, GROUP_W)


def _dot_hi_lo(a, b, f32_side):
    x = a if f32_side == 0 else b
    hi = x.astype(BF16)
    lo = (x - hi.astype(F32)).astype(BF16)
    if f32_side == 0:
        return _dot(hi, b) + _dot(lo, b)
    return _dot(a, hi) + _dot(a, lo)


def _out_proj_kernel(x_ref, o1_ref, l1_ref, o4_ref, l4_ref, o16_ref, l16_ref, ob_ref, wo_ref,
                     pt4_ref, pt16_ref, expand_ref, y_ref):
    o1, l1 = o1_ref[...].astype(F32), l1_ref[...]
    o2 = _dot(pt4_ref[...], o4_ref[...].reshape(PERM_ROWS, GROUP_W))
    o3 = _dot(pt16_ref[...], o16_ref[...].reshape(PERM_ROWS, GROUP_W))
    l2 = _dot_hi_lo(pt4_ref[...], l4_ref[...].reshape(PERM_ROWS, LANES), 1)
    l3 = _dot_hi_lo(pt16_ref[...], l16_ref[...].reshape(PERM_ROWS, LANES), 1)
    m = jnp.maximum(jnp.maximum(l1, l2), l3)
    e1, e2, e3 = jnp.exp(l1 - m), jnp.exp(l2 - m), jnp.exp(l3 - m)
    inv = 1.0 / (e1 + e2 + e3)
    w1, w2, w3 = (_dot_hi_lo(e * inv, expand_ref[...], 0) for e in (e1, e2, e3))
    out_a = w1 * o1 + w2 * o2 + w3 * o3
    y = x_ref[...] + _dot(out_a.astype(BF16), wo_ref[0:GROUP_W, :])
    y_ref[...] = y + _dot(ob_ref[...], wo_ref[GROUP_W:2 * GROUP_W, :])


def _out_proj(x2, branches, ob, w_o, t):
    n, d = x2.shape
    tm = PERM_ROWS
    tiles = t // tm
    row = lambda i: (i, 0)
    const = lambda i: (0, 0)
    res_map = lambda i: (i // tiles, 0, i % tiles, 0)
    ospec = lambda dil: pl.BlockSpec((None, dil, tm // dil, GROUP_W), res_map)
    lspec = lambda dil: pl.BlockSpec((None, dil, tm // dil, LANES), res_map)
    pspec = pl.BlockSpec((tm, tm), const)
    (o1, l1), (o4, l4), (o16, l16) = branches
    expand = np.zeros((LANES, GROUP_W), np.float32)
    for h in range(N_HEADS):
        expand[h, h * HEAD_DIM:(h + 1) * HEAD_DIM] = 1.0
    return pl.pallas_call(
        _out_proj_kernel,
        grid=(n // tm,),
        in_specs=[pl.BlockSpec((tm, d), row), pl.BlockSpec((tm, GROUP_W), row),
                  pl.BlockSpec((tm, LANES), row), ospec(4), lspec(4), ospec(16), lspec(16),
                  pl.BlockSpec((tm, GROUP_W), row), pl.BlockSpec((2 * GROUP_W, d), const), pspec, pspec,
                  pl.BlockSpec((LANES, GROUP_W), const)],
        out_specs=pl.BlockSpec((tm, d), row),
        out_shape=jax.ShapeDtypeStruct((n, d), F32),
        compiler_params=_cparams(1), name="merge_out_proj",
    )(x2, o1.reshape(n, GROUP_W), l1.reshape(n, LANES), o4, l4, o16, l16, ob, w_o.astype(BF16),
      jnp.asarray(_residue_perm(4).T, BF16), jnp.asarray(_residue_perm(16).T, BF16),
      jnp.asarray(expand, BF16))


HALO = 8


def _ffn_kernel(x_ref, g_ref, wup_ref, cw_ref, cb_ref, wdn_ref, p_ref, gp_ref, wg_ref, wp_ref, gf_ref,
                y_ref, acc_ref, buf_ref, tail_ref, *, tiles_per_seq, final):
    tm = x_ref.shape[0]
    i = pl.program_id(0)

    @pl.when(i % tiles_per_seq == 0)
    def _():
        tail_ref[...] = jnp.zeros(tail_ref.shape, F32)

    x = x_ref[...]
    h = _rms(x, g_ref[...]).astype(BF16)
    acc_ref[...] = x

    def conv(c):
        cols = slice(c * FF_CHUNK, (c + 1) * FF_CHUNK)
        u = _dot(h, wup_ref[:, cols])
        buf_ref[0:HALO, :] = tail_ref[c]
        buf_ref[HALO:HALO + tm, :] = u
        tail_ref[c] = u[tm - HALO:, :]
        y = cb_ref[:, cols] + cw_ref[2:3, cols] * u
        y = y + cw_ref[1:2, cols] * buf_ref[HALO - 1:HALO - 1 + tm, :]
        return y + cw_ref[0:1, cols] * buf_ref[HALO - 2:HALO - 2 + tm, :]

    for c in range(N_FF_CHUNKS):
        gate = conv(c)
        up = conv(c + N_FF_CHUNKS)
        act = (gate * jax.nn.sigmoid(gate) * up).astype(BF16)
        acc_ref[...] += _dot(act, wdn_ref[c * FF_CHUNK:(c + 1) * FF_CHUNK, :])
    x = acc_ref[...]
    gate = jax.nn.sigmoid(_dot(_rms(x, gp_ref[...]).astype(BF16), wg_ref[...]))
    x = x + gate * _dot(p_ref[...].astype(BF16), wp_ref[...])
    y_ref[...] = _rms(x, gf_ref[...]) if final else x


def _ffn_ple(x2, t, ffn_norm, w_up, conv_w, conv_b, w_down, p2, ple_norm, w_gate, w_proj,
             final_norm, final, tm):
    n, d = x2.shape
    pd = p2.shape[1]
    nc2 = 2 * N_FF_CHUNKS
    row = lambda i: (i, 0)
    const = lambda i: (0, 0)
    once = pl.Buffered(1)
    return pl.pallas_call(
        functools.partial(_ffn_kernel, tiles_per_seq=t // tm, final=final),
        grid=(n // tm,),
        in_specs=[pl.BlockSpec((tm, d), row), pl.BlockSpec((1, d), const),
                  pl.BlockSpec((d, 2 * D_FF), const, pipeline_mode=once),
                  pl.BlockSpec((CONV_WIDTH, 2 * D_FF), const),
                  pl.BlockSpec((1, 2 * D_FF), const),
                  pl.BlockSpec((D_FF, d), const, pipeline_mode=once),
                  pl.BlockSpec((tm, pd), row), pl.BlockSpec((1, d), const),
                  pl.BlockSpec((d, d), const, pipeline_mode=once),
                  pl.BlockSpec((pd, d), const, pipeline_mode=once),
                  pl.BlockSpec((1, d), const)],
        out_specs=pl.BlockSpec((tm, d), row),
        out_shape=jax.ShapeDtypeStruct((n, d), F32),
        scratch_shapes=[pltpu.VMEM((tm, d), F32), pltpu.VMEM((tm + HALO, FF_CHUNK), F32),
                        pltpu.VMEM((nc2, HALO, FF_CHUNK), F32)],
        compiler_params=_cparams(1), name="conv_ffn_ple",
    )(x2, ffn_norm.reshape(1, d), w_up.astype(BF16), conv_w, conv_b.reshape(1, 2 * D_FF),
      w_down.astype(BF16), p2, ple_norm.reshape(1, d),
      w_gate.astype(BF16), w_proj.astype(BF16), final_norm.reshape(1, d))


PROJ_TM = 512


def kernel(x, p, positions, attn_norm, w_in, w_o, ffn_norm, w_up, conv_w, conv_b, w_down,
           ple_norm, w_ple_gate, w_ple_proj, final_norm):
    b, t, d = x.shape
    depth = p.shape[0]
    n = b * t
    topk = min(TOPK_MAX, t // 4)
    assert t % DILATED_PAIRS[-1][0] == 0 and t % DSA_TQ == 0 and t % PROJ_TM == 0
    x2 = x.reshape(n, d)
    pos2 = positions.reshape(n, 1)
    for i in range(depth):
        za, zb, zi, wi, za4, za16 = _project(x2, pos2, attn_norm[i], w_in[i], b, t, PROJ_TM)
        views = (za.reshape(b, 1, t, 3 * GROUP_W), za4, za16)
        branches = [_dilated_branch(zr, dil) for zr, (_, dil) in zip(views, DILATED_PAIRS)]
        ob = _sparse_attention(zb, zi, wi, b, t, topk)
        x2 = _out_proj(x2, branches, ob, w_o[i], t)
        x2 = _ffn_ple(x2, t, ffn_norm[i], w_up[i], conv_w[i], conv_b[i], w_down[i],
                      p[i].reshape(n, -1), ple_norm[i], w_ple_gate[i], w_ple_proj[i],
                      final_norm, i == depth - 1, PROJ_TM)
    return x2.reshape(b, t, d)
```

```python
import functools
import math

import numpy as np
import jax
import jax.numpy as jnp
from jax import lax
from jax.experimental import pallas as pl
from jax.experimental.pallas import tpu as pltpu

F32 = jnp.float32
BF16 = jnp.bfloat16

HEAD_DIM = 64
HALF = HEAD_DIM // 2
N_HEADS = 8
GROUP_W = N_HEADS * HEAD_DIM
N_PAIRS = N_HEADS // 2
LANES = 128
DILATED_PAIRS = ((128, 1), (512, 4), (2048, 16))
WIN_STEPS = 128
TOPK_MAX = 256
D_FF = 2816
FF_CHUNK = 256
N_FF_CHUNKS = D_FF // FF_CHUNK
CONV_WIDTH = 3
ROPE_THETA = 10000.0
RMS_EPS = 1e-6
MASK_VALUE = -1e30
KEY_NEG_INF = -2139095041
VMEM_LIMIT = 56 * 1024 * 1024


def _cparams(n_axes):
    return pltpu.CompilerParams(dimension_semantics=("arbitrary",) * n_axes,
                                vmem_limit_bytes=VMEM_LIMIT)


def _dot_t(a, b):
    return lax.dot_general(a, b, (((1,), (1,)), ((), ())), preferred_element_type=F32)


def _dot(a, b):
    return jnp.dot(a, b, preferred_element_type=F32)


def _rms(x, g):
    ms = jnp.mean(x * x, axis=-1, keepdims=True)
    return (x * lax.rsqrt(ms + RMS_EPS)) * g


def _pair_perm():
    idx = []
    for j in range(N_PAIRS):
        a, b = 2 * j * HEAD_DIM, (2 * j + 1) * HEAD_DIM
        idx += list(range(a, a + HALF)) + list(range(b, b + HALF))
        idx += list(range(a + HALF, a + HEAD_DIM)) + list(range(b + HALF, b + HEAD_DIM))
    return np.asarray(idx, np.int32)


def _in_proj_columns():
    perm = _pair_perm()
    nat = np.arange(GROUP_W, dtype=np.int32)
    cols = []
    for g, roped in enumerate((True, True, False, True, True, False, True)):
        cols.append(g * GROUP_W + (perm if roped else nat))
    kbase = 7 * GROUP_W
    k1 = kbase + np.arange(HALF, dtype=np.int32)
    k2 = kbase + HALF + np.arange(HALF, dtype=np.int32)
    cols.append(np.concatenate([k1, k1, k2, k2]))
    return np.concatenate(cols)


PERM_ROWS = 256


def _residue_perm(dil):
    p = np.zeros((PERM_ROWS, PERM_ROWS), np.float32)
    per = PERM_ROWS // dil
    for r in range(dil):
        for u in range(per):
            p[r * per + u, u * dil + r] = 1.0
    return p


def _proj_kernel(x_ref, pos_ref, g_ref, invf_ref, w_ref, wiw_ref, p4_ref, p16_ref,
                 za_ref, zb_ref, zi_ref, wi_ref, za4_ref, za16_ref):
    x = x_ref[...]
    h32 = _rms(x, g_ref[...])
    h = h32.astype(BF16)
    ang = pos_ref[...].astype(F32) * invf_ref[...]
    lane = lax.broadcasted_iota(jnp.int32, (1, LANES), 1)
    cos = jnp.cos(ang)
    sin = jnp.sin(ang) * jnp.where(lane < 2 * HALF, -1.0, 1.0)

    def rope(z):
        return z * cos + pltpu.roll(z, 2 * HALF, 1) * sin

    q_scale = HEAD_DIM ** -0.5
    plan = ((za_ref, 0, True, q_scale), (za_ref, 1, True, 1.0), (za_ref, 2, False, 1.0),
            (zb_ref, 0, True, q_scale * math.log2(math.e)), (zb_ref, 1, True, 1.0), (zb_ref, 2, False, 1.0),
            (zi_ref, 0, True, q_scale))
    for g, (out_ref, slot, roped, scale) in enumerate(plan):
        z = _dot(h, w_ref[:, g * GROUP_W:(g + 1) * GROUP_W])
        for j in range(N_PAIRS):
            zj = z[:, j * LANES:(j + 1) * LANES]
            if roped:
                zj = rope(zj)
            if scale != 1.0:
                zj = zj * scale
            out_ref[:, slot * GROUP_W + j * LANES: slot * GROUP_W + (j + 1) * LANES] = zj.astype(BF16)
    zk = _dot(h, w_ref[:, 7 * GROUP_W: 7 * GROUP_W + LANES])
    zi_ref[:, GROUP_W:GROUP_W + LANES] = rope(zk).astype(BF16)
    wi_ref[...] = _dot(h, wiw_ref[...]) * (N_HEADS ** -0.5)
    for half in range(x.shape[0] // PERM_ROWS):
        za = za_ref[half * PERM_ROWS:(half + 1) * PERM_ROWS, :]
        for dil, p_ref, out_ref in ((4, p4_ref, za4_ref), (16, p16_ref, za16_ref)):
            per = PERM_ROWS // dil
            zp = _dot(p_ref[...], za).astype(BF16)
            for r in range(dil):
                out_ref[r, half * per:(half + 1) * per, :] = zp[r * per:(r + 1) * per, :]


def _project(x2, pos2, attn_norm, w_in, b, t, tm):
    n, d = x2.shape
    tiles = t // tm
    res_map = lambda i: (i // tiles, 0, i % tiles, 0)
    cols = _in_proj_columns()
    w_main = jnp.take(w_in, cols, axis=1).astype(BF16)
    w_wi = jnp.pad(w_in[:, 7 * GROUP_W + HEAD_DIM:], ((0, 0), (0, LANES - N_HEADS))).astype(BF16)
    inv_freq = 1.0 / (ROPE_THETA ** (np.arange(0, HEAD_DIM, 2, dtype=np.float32) / HEAD_DIM))
    invf = jnp.asarray(np.tile(inv_freq, 4)[None, :], F32)
    wcols = w_main.shape[1]
    const = lambda i: (0, 0)
    row = lambda i: (i, 0)
    return pl.pallas_call(
        _proj_kernel,
        grid=(n // tm,),
        in_specs=[pl.BlockSpec((tm, d), row), pl.BlockSpec((tm, 1), row),
                  pl.BlockSpec((1, d), const), pl.BlockSpec((1, LANES), const),
                  pl.BlockSpec((d, wcols), const, pipeline_mode=pl.Buffered(1)),
                  pl.BlockSpec((d, LANES), const),
                  pl.BlockSpec((PERM_ROWS, PERM_ROWS), const),
                  pl.BlockSpec((PERM_ROWS, PERM_ROWS), const)],
        out_specs=[pl.BlockSpec((tm, 3 * GROUP_W), row), pl.BlockSpec((tm, 3 * GROUP_W), row),
                   pl.BlockSpec((tm, GROUP_W + LANES), row), pl.BlockSpec((tm, LANES), row),
                   pl.BlockSpec((None, 4, tm // 4, 3 * GROUP_W), res_map),
                   pl.BlockSpec((None, 16, tm // 16, 3 * GROUP_W), res_map)],
        out_shape=[jax.ShapeDtypeStruct((n, 3 * GROUP_W), BF16),
                   jax.ShapeDtypeStruct((n, 3 * GROUP_W), BF16),
                   jax.ShapeDtypeStruct((n, GROUP_W + LANES), BF16),
                   jax.ShapeDtypeStruct((n, LANES), F32),
                   jax.ShapeDtypeStruct((b, 4, t // 4, 3 * GROUP_W), BF16),
                   jax.ShapeDtypeStruct((b, 16, t // 16, 3 * GROUP_W), BF16)],
        compiler_params=_cparams(1), name="in_proj_rope",
    )(x2, pos2, attn_norm.reshape(1, d), invf, w_main, w_wi,
      jnp.asarray(_residue_perm(4), BF16), jnp.asarray(_residue_perm(16), BF16))


def _head_lane_masks():
    lane = lax.broadcasted_iota(jnp.int32, (1, LANES), 1)
    first = (lane % HEAD_DIM) < HALF
    return first, jnp.logical_not(first)


def _dilated_kernel(q_ref, kc_ref, kp_ref, vc_ref, vp_ref, o_ref, lse_ref, s_ref):
    nb = pl.program_id(2)
    n = WIN_STEPS
    a = lax.broadcasted_iota(jnp.int32, (n, 2 * n), 0)
    c = lax.broadcasted_iota(jnp.int32, (n, 2 * n), 1)
    band = (c >= a) & (c <= a + n) & ((nb > 0) | (c >= n))
    masks = _head_lane_masks()
    lane = lax.broadcasted_iota(jnp.int32, (1, LANES), 1)
    low = lane < HEAD_DIM
    for h in range(N_HEADS):
        j, hb = divmod(h, 2)
        sl = slice(j * LANES, (j + 1) * LANES)
        q = q_ref[:, sl]
        kk = jnp.concatenate([kp_ref[:, sl], kc_ref[:, sl]], axis=0)
        qm = jnp.where(masks[hb], q, jnp.zeros_like(q))
        s_ref[h] = jnp.where(band, _dot_t(qm, kk), -jnp.inf)
    lse_tile = jnp.zeros((n, LANES), F32)
    for j in range(N_PAIRS):
        sl = slice(j * LANES, (j + 1) * LANES)
        vv = jnp.concatenate([vp_ref[:, sl], vc_ref[:, sl]], axis=0)
        outs = []
        for hb in range(2):
            s = s_ref[2 * j + hb]
            m = jnp.max(s, axis=1, keepdims=True)
            p = jnp.exp(s - m)
            l = jnp.sum(p, axis=1, keepdims=True)
            outs.append(_dot(p.astype(BF16), vv) / l)
            lse_tile = jnp.where(lane == 2 * j + hb, m + jnp.log(l), lse_tile)
        o_ref[:, sl] = jnp.where(low, outs[0], outs[1]).astype(BF16)
    lse_ref[...] = lse_tile


def _dilated_branch(zr, dil):
    n = WIN_STEPS
    b, _, rows, _ = zr.shape
    blk = (None, None, n, GROUP_W)
    cur = lambda slot: (lambda bi, r, nb: (bi, r, nb, slot))
    prev = lambda slot: (lambda bi, r, nb: (bi, r, jnp.maximum(nb - 1, 0), slot))
    return pl.pallas_call(
        _dilated_kernel,
        grid=(b, dil, rows // n),
        in_specs=[pl.BlockSpec(blk, cur(0)), pl.BlockSpec(blk, cur(1)), pl.BlockSpec(blk, prev(1)),
                  pl.BlockSpec(blk, cur(2)), pl.BlockSpec(blk, prev(2))],
        out_specs=[pl.BlockSpec(blk, cur(0)), pl.BlockSpec((None, None, n, LANES), cur(0))],
        out_shape=[jax.ShapeDtypeStruct((b, dil, rows, GROUP_W), BF16),
                   jax.ShapeDtypeStruct((b, dil, rows, LANES), F32)],
        scratch_shapes=[pltpu.VMEM((N_HEADS, n, 2 * n), F32)],
        compiler_params=_cparams(3), name=f"dilated_attn_d{dil}",
    )(zr, zr, zr, zr, zr)


DSA_TQ = 256
P1_CHUNK = 256
P3_CHUNK = 256
CNT_ROWS = 32
BF16_ROWS = 16
VT_ROWS = HEAD_DIM + BF16_ROWS
BF16_MIN_NORMAL = 0x0080
CHECK_EVERY = 2


def _chunk_loop(n, body, init):
    carry = lax.fori_loop(0, n // 2, lambda i, c: body(2 * i + 1, body(2 * i, c)), init)
    return lax.cond(n % 2 == 1, lambda c: body(n - 1, c), lambda c: c, carry)


def _to_key(x):
    bits = pltpu.bitcast(x, jnp.int32)
    return bits ^ ((bits >> 31) & 0x7FFFFFFF)


def _dsa_kernel(qb_ref, qi_ref, wit_ref, kb_ref, vt_ref, ki_ref, ot_ref,
                key_ref, top_ref, qim_ref, qbm_ref, m_ref, acc_ref, thr_ref, s_ref, mx_ref,
                *, topk):
    assert P3_CHUNK == P1_CHUNK == DSA_TQ
    tq = DSA_TQ
    qblk = pl.program_id(1)
    t0 = qblk * tq
    n1 = qblk + 1
    masks = _head_lane_masks()

    for h in range(N_HEADS):
        j, hb = divmod(h, 2)
        sl = slice(j * LANES, (j + 1) * LANES)
        qi = qi_ref[:, sl]
        qb = qb_ref[:, sl]
        qim_ref[h] = jnp.where(masks[hb], qi, jnp.zeros_like(qi))
        qbm_ref[h] = jnp.where(masks[hb], qb, jnp.zeros_like(qb))

    def p1_chunk(c, diagonal):
        k0 = pl.multiple_of(c * P1_CHUNK, P1_CHUNK)
        kic = ki_ref[pl.ds(k0, P1_CHUNK), :]
        acc = jnp.zeros((P1_CHUNK, tq), F32)
        for h in range(N_HEADS):
            acc = acc + wit_ref[h:h + 1, :] * jnp.maximum(_dot_t(kic, qim_ref[h]), 0.0)
        acc = jnp.where(acc == 0.0, 0.0, acc)
        if diagonal:
            kpos = lax.broadcasted_iota(jnp.int32, (P1_CHUNK, tq), 0)
            qpos = lax.broadcasted_iota(jnp.int32, (P1_CHUNK, tq), 1)
            acc = jnp.where(kpos <= qpos, acc, -jnp.inf)
        key_ref[pl.ds(k0, P1_CHUNK), :] = _to_key(acc)
        top = pltpu.bitcast(pltpu.bitcast(acc, jnp.int32) & (-65536), F32)
        top_ref[pl.ds(k0, P1_CHUNK), :] = top.astype(BF16)

    def p1_pair(i, carry):
        p1_chunk(2 * i, False)
        p1_chunk(2 * i + 1, False)
        return carry

    lax.fori_loop(0, qblk // 2, p1_pair, 0)

    @pl.when(qblk % 2 == 1)
    def _():
        p1_chunk(qblk - 1, False)

    p1_chunk(qblk, True)

    def count_ge(thr):
        def body(c, cnt):
            k0 = pl.multiple_of(c * P1_CHUNK, P1_CHUNK)
            x = jnp.where(key_ref[pl.ds(k0, P1_CHUNK), :] >= thr, 1, 0)
            return cnt + jnp.sum(x.reshape(P1_CHUNK // CNT_ROWS, CNT_ROWS, tq), axis=0)

        cnt = _chunk_loop(n1, body, jnp.zeros((CNT_ROWS, tq), jnp.int32))
        return jnp.sum(cnt, axis=0, keepdims=True)

    def count_packed(thr_f):
        thr_b = jnp.broadcast_to(thr_f, (BF16_ROWS, tq)).astype(BF16)[None]
        one, zero = jnp.ones((), BF16), jnp.zeros((), BF16)

        def body(c, cnt):
            k0 = pl.multiple_of(c * P1_CHUNK, P1_CHUNK)
            x = top_ref[pl.ds(k0, P1_CHUNK), :].reshape(P1_CHUNK // BF16_ROWS, BF16_ROWS, tq)
            y = jnp.where(x >= thr_b, one, zero)
            parts = [y[r] for r in range(P1_CHUNK // BF16_ROWS)]
            while len(parts) > 1:
                parts = [a + b for a, b in zip(parts[::2], parts[1::2])]
            return cnt + parts[0].astype(F32)

        cnt = _chunk_loop(n1, body, jnp.zeros((BF16_ROWS, tq), F32))
        return jnp.sum(cnt, axis=0, keepdims=True).astype(jnp.int32)

    def count_ge_top(mid16):
        pattern = mid16 ^ ((mid16 >> 15) & 0x7FFF)
        pattern = jnp.where((mid16 > 0) & (mid16 < BF16_MIN_NORMAL), BF16_MIN_NORMAL, pattern)
        return count_packed(pltpu.bitcast(pattern << 16, F32))

    def step(st, count_fn, final):
        lo, hi, thr, done, cnt_hi = st
        mid = (lo >> 1) + (hi >> 1) + (((lo & 1) + (hi & 1) + 1) >> 1)
        c = count_fn(mid)
        ge = c >= topk
        active = done == 0
        hit = active & (c == topk)
        new_lo = jnp.where(active & ge, mid, lo)
        new_hi = jnp.where(active & jnp.logical_not(ge), mid - 1, hi)
        new_cnt_hi = jnp.where(active & jnp.logical_not(ge), c, cnt_hi)
        new_thr = jnp.where(hit, mid, thr)
        new_done = jnp.where(hit, 1, done)
        if final:
            conv = active & jnp.logical_not(hit) & (new_lo == new_hi)
            new_thr = jnp.where(conv, new_lo, new_thr)
            new_done = jnp.where(conv, 2, new_done)
        return new_lo, new_hi, new_thr, new_done, new_cnt_hi

    t_row = t0 + lax.broadcasted_iota(jnp.int32, (1, tq), 1)
    all_sel = t_row + 1 <= topk
    done0 = all_sel.astype(jnp.int32)
    zeros = jnp.zeros((1, tq), jnp.int32)
    lo16 = jnp.full((1, tq), (KEY_NEG_INF + 1) >> 16, jnp.int32)
    hi16 = jnp.full((1, tq), 0x7F7F, jnp.int32)
    st = lax.fori_loop(0, 16, lambda _, s: step(s, count_ge_top, False),
                       (lo16, hi16, zeros, done0, zeros))
    lo16, _, thr16, done16, cnt_hi = st
    def digits(c, carry):
        k0 = pl.multiple_of(c * P1_CHUNK, P1_CHUNK)
        key = key_ref[pl.ds(k0, P1_CHUNK), :]
        digit = jnp.where((key >> 16) == lo16, (key >> 8) & 0xFF, -1)
        top_ref[pl.ds(k0, P1_CHUNK), :] = digit.astype(F32).astype(BF16)
        return carry

    _chunk_loop(n1, digits, 0)
    above = cnt_hi
    st = lax.fori_loop(0, 8, lambda _, s: step(s, lambda m: above + count_packed(m.astype(F32)), False),
                       (zeros, zeros + 0xFF, zeros, done16, cnt_hi))
    lo8, _, thr8, done, cnt_hi = st
    lo = (lo16 << 16) | (lo8 << 8)
    thr = jnp.where(done16 == 1, thr16 << 16, (lo16 << 16) | (thr8 << 8))
    thr = jnp.where(all_sel, KEY_NEG_INF + 1, thr)

    def fine_cond(carry):
        it, st = carry
        return (it < 8 // CHECK_EVERY) & (jnp.min(st[3]) == 0)

    def fine_body(carry):
        it, st = carry
        return it + 1, lax.fori_loop(0, CHECK_EVERY, lambda _, s: step(s, count_ge, True), st)

    _, st = lax.while_loop(fine_cond, fine_body, (0, (lo, lo + 0xFF, thr, done, cnt_hi)))
    _, _, thr, done, cnt_hi = st
    thr_ref[...] = thr

    tie = done == 2

    @pl.when(jnp.max(done) == 2)
    def _():
        need = topk - cnt_hi
        sub_i = lax.broadcasted_iota(jnp.int32, (P1_CHUNK, tq), 0)

        def count_eq_le(jmax):
            def body(c, cnt):
                k0 = pl.multiple_of(c * P1_CHUNK, P1_CHUNK)
                kc = key_ref[pl.ds(k0, P1_CHUNK), :]
                x = jnp.where((kc == thr) & (sub_i + k0 <= jmax), 1, 0)
                return cnt + jnp.sum(x.reshape(P1_CHUNK // CNT_ROWS, CNT_ROWS, tq), axis=0)

            cnt = lax.fori_loop(0, n1, body, jnp.zeros((CNT_ROWS, tq), jnp.int32))
            return jnp.sum(cnt, axis=0, keepdims=True)

        def jbody(_, st):
            jlo, jhi = st
            jmid = (jlo + jhi) >> 1
            ok = count_eq_le(jmid) >= need
            return jnp.where(ok, jlo, jmid + 1), jnp.where(ok, jmid, jhi)

        jlo, _ = lax.fori_loop(0, 14, jbody, (jnp.zeros((1, tq), jnp.int32),
                                               jnp.broadcast_to(t0 + tq - 1, (1, tq)).astype(jnp.int32)))
        jcut = jnp.where(tie, jlo, 0x7FFFFFFF)

        def demote(c, carry):
            k0 = pl.multiple_of(c * P1_CHUNK, P1_CHUNK)
            kc = key_ref[pl.ds(k0, P1_CHUNK), :]
            key_ref[pl.ds(k0, P1_CHUNK), :] = jnp.where((kc == thr) & (sub_i + k0 > jcut), kc - 1, kc)
            return carry

        lax.fori_loop(0, n1, demote, 0)

    m_ref[...] = jnp.full(m_ref.shape, MASK_VALUE, F32)
    acc_ref[...] = jnp.zeros(acc_ref.shape, F32)

    def scores(c, slot):
        k0 = pl.multiple_of(c * P3_CHUNK, P3_CHUNK)
        sel = key_ref[pl.ds(k0, P3_CHUNK), :] >= thr_ref[...]
        for h in range(N_HEADS):
            j = h // 2
            kc = kb_ref[pl.ds(k0, P3_CHUNK), j * LANES:(j + 1) * LANES]
            s = jnp.where(sel, _dot_t(kc, qbm_ref[h]), MASK_VALUE)
            s_ref[slot, h] = s
            mx_ref[slot, h] = jnp.max(s, axis=0, keepdims=True)

    def accumulate(c, slot):
        k0 = pl.multiple_of(c * P3_CHUNK, P3_CHUNK)
        for h in range(N_HEADS):
            rows = slice(h * VT_ROWS, (h + 1) * VT_ROWS)
            m_old = m_ref[h]
            m_new = jnp.maximum(m_old, mx_ref[slot, h])
            alpha = jnp.exp2(m_old - m_new)
            p = jnp.exp2((s_ref[slot, h] - m_new).astype(BF16))
            m_ref[h] = m_new
            pv = _dot(vt_ref[rows, pl.ds(k0, P3_CHUNK)], p)
            acc_ref[rows, :] = alpha * acc_ref[rows, :] + pv

    scores(0, 0)

    def p3_pair(i, carry):
        c = 2 * i
        scores(c + 1, 1)
        accumulate(c, 0)
        scores(c + 2, 0)
        accumulate(c + 1, 1)
        return carry

    lax.fori_loop(0, (n1 - 1) // 2, p3_pair, 0)

    @pl.when(n1 % 2 == 1)
    def _():
        accumulate(n1 - 1, 0)

    @pl.when(n1 % 2 == 0)
    def _():
        scores(n1 - 1, 1)
        accumulate(n1 - 2, 0)
        accumulate(n1 - 1, 1)

    for h in range(N_HEADS):
        base = h * VT_ROWS
        denom = acc_ref[base + HEAD_DIM:base + HEAD_DIM + 1, :]
        ot_ref[h * HEAD_DIM:(h + 1) * HEAD_DIM, :] = (acc_ref[base:base + HEAD_DIM, :] / denom).astype(BF16)


def _sparse_attention(zb, zi, wi, b, t, topk):
    tq = DSA_TQ
    zb3 = zb.reshape(b, t, 3 * GROUP_W)
    zi3 = zi.reshape(b, t, GROUP_W + LANES)
    vt = zb3[:, :, 2 * GROUP_W:].transpose(0, 2, 1).reshape(b, N_HEADS, HEAD_DIM, t)
    pad = jnp.zeros((b, N_HEADS, VT_ROWS - HEAD_DIM, t), BF16).at[:, :, 0, :].set(1.0)
    vt = jnp.concatenate([vt, pad], axis=2).reshape(b, N_HEADS * VT_ROWS, t)
    wit = wi.reshape(b, t, LANES)[:, :, :N_HEADS].transpose(0, 2, 1)
    qmap = lambda bi, qi: (bi, qi, 0)
    tmap = lambda bi, qi: (bi, 0, qi)
    out_t = pl.pallas_call(
        functools.partial(_dsa_kernel, topk=topk),
        grid=(b, t // tq),
        in_specs=[pl.BlockSpec((None, tq, GROUP_W), qmap),
                  pl.BlockSpec((None, tq, GROUP_W), qmap),
                  pl.BlockSpec((None, N_HEADS, tq), tmap),
                  pl.BlockSpec((None, t, GROUP_W), lambda bi, qi: (bi, 0, 1),
                               pipeline_mode=pl.Buffered(1)),
                  pl.BlockSpec((None, N_HEADS * VT_ROWS, t), lambda bi, qi: (bi, 0, 0),
                               pipeline_mode=pl.Buffered(1)),
                  pl.BlockSpec((None, t, LANES), lambda bi, qi: (bi, 0, GROUP_W // LANES),
                               pipeline_mode=pl.Buffered(1))],
        out_specs=pl.BlockSpec((None, GROUP_W, tq), tmap),
        out_shape=jax.ShapeDtypeStruct((b, GROUP_W, t), BF16),
        scratch_shapes=[pltpu.VMEM((t, tq), jnp.int32),
                        pltpu.VMEM((t, tq), BF16),
                        pltpu.VMEM((N_HEADS, tq, LANES), BF16),
                        pltpu.VMEM((N_HEADS, tq, LANES), BF16),
                        pltpu.VMEM((N_HEADS, 1, tq), F32),
                        pltpu.VMEM((N_HEADS * VT_ROWS, tq), F32),
                        pltpu.VMEM((1, tq), jnp.int32),
                        pltpu.VMEM((2, N_HEADS, P3_CHUNK, tq), F32),
                        pltpu.VMEM((2, N_HEADS, 1, tq), F32)],
        compiler_params=_cparams(2), name="dsa_attn",
    )(zb3, zi3, wit, zb3, vt, zi3)
    return out_t.transpose(0, 2, 1).reshape(b * t, GROUP_W)


def _dot_hi_lo(a, b, f32_side):
    x = a if f32_side == 0 else b
    hi = x.astype(BF16)
    lo = (x - hi.astype(F32)).astype(BF16)
    if f32_side == 0:
        return _dot(hi, b) + _dot(lo, b)
    return _dot(a, hi) + _dot(a, lo)


def _out_proj_kernel(x_ref, o1_ref, l1_ref, o4_ref, l4_ref, o16_ref, l16_ref, ob_ref, wo_ref,
                     pt4_ref, pt16_ref, expand_ref, y_ref):
    o1, l1 = o1_ref[...].astype(F32), l1_ref[...]
    o2 = _dot(pt4_ref[...], o4_ref[...].reshape(PERM_ROWS, GROUP_W))
    o3 = _dot(pt16_ref[...], o16_ref[...].reshape(PERM_ROWS, GROUP_W))
    l2 = _dot_hi_lo(pt4_ref[...], l4_ref[...].reshape(PERM_ROWS, LANES), 1)
    l3 = _dot_hi_lo(pt16_ref[...], l16_ref[...].reshape(PERM_ROWS, LANES), 1)
    m = jnp.maximum(jnp.maximum(l1, l2), l3)
    e1, e2, e3 = jnp.exp(l1 - m), jnp.exp(l2 - m), jnp.exp(l3 - m)
    inv = 1.0 / (e1 + e2 + e3)
    w1, w2, w3 = (_dot_hi_lo(e * inv, expand_ref[...], 0) for e in (e1, e2, e3))
    out_a = w1 * o1 + w2 * o2 + w3 * o3
    y = x_ref[...] + _dot(out_a.astype(BF16), wo_ref[0:GROUP_W, :])
    y_ref[...] = y + _dot(ob_ref[...], wo_ref[GROUP_W:2 * GROUP_W, :])


def _out_proj(x2, branches, ob, w_o, t):
    n, d = x2.shape
    tm = PERM_ROWS
    tiles = t // tm
    row = lambda i: (i, 0)
    const = lambda i: (0, 0)
    res_map = lambda i: (i // tiles, 0, i % tiles, 0)
    ospec = lambda dil: pl.BlockSpec((None, dil, tm // dil, GROUP_W), res_map)
    lspec = lambda dil: pl.BlockSpec((None, dil, tm // dil, LANES), res_map)
    pspec = pl.BlockSpec((tm, tm), const)
    (o1, l1), (o4, l4), (o16, l16) = branches
    expand = np.zeros((LANES, GROUP_W), np.float32)
    for h in range(N_HEADS):
        expand[h, h * HEAD_DIM:(h + 1) * HEAD_DIM] = 1.0
    return pl.pallas_call(
        _out_proj_kernel,
        grid=(n // tm,),
        in_specs=[pl.BlockSpec((tm, d), row), pl.BlockSpec((tm, GROUP_W), row),
                  pl.BlockSpec((tm, LANES), row), ospec(4), lspec(4), ospec(16), lspec(16),
                  pl.BlockSpec((tm, GROUP_W), row), pl.BlockSpec((2 * GROUP_W, d), const), pspec, pspec,
                  pl.BlockSpec((LANES, GROUP_W), const)],
        out_specs=pl.BlockSpec((tm, d), row),
        out_shape=jax.ShapeDtypeStruct((n, d), F32),
        compiler_params=_cparams(1), name="merge_out_proj",
    )(x2, o1.reshape(n, GROUP_W), l1.reshape(n, LANES), o4, l4, o16, l16, ob, w_o.astype(BF16),
      jnp.asarray(_residue_perm(4).T, BF16), jnp.asarray(_residue_perm(16).T, BF16),
      jnp.asarray(expand, BF16))


HALO = 8


def _ffn_kernel(x_ref, g_ref, wup_ref, cw_ref, cb_ref, wdn_ref, p_ref, gp_ref, wg_ref, wp_ref, gf_ref,
                y_ref, acc_ref, buf_ref, tail_ref, *, tiles_per_seq, final):
    tm = x_ref.shape[0]
    i = pl.program_id(0)

    @pl.when(i % tiles_per_seq == 0)
    def _():
        tail_ref[...] = jnp.zeros(tail_ref.shape, F32)

    x = x_ref[...]
    h = _rms(x, g_ref[...]).astype(BF16)
    acc_ref[...] = x

    def conv(c):
        cols = slice(c * FF_CHUNK, (c + 1) * FF_CHUNK)
        u = _dot(h, wup_ref[:, cols])
        buf_ref[0:HALO, :] = tail_ref[c]
        buf_ref[HALO:HALO + tm, :] = u
        tail_ref[c] = u[tm - HALO:, :]
        y = cb_ref[:, cols] + cw_ref[2:3, cols] * u
        y = y + cw_ref[1:2, cols] * buf_ref[HALO - 1:HALO - 1 + tm, :]
        return y + cw_ref[0:1, cols] * buf_ref[HALO - 2:HALO - 2 + tm, :]

    for c in range(N_FF_CHUNKS):
        gate = conv(c)
        up = conv(c + N_FF_CHUNKS)
        act = (gate * jax.nn.sigmoid(gate) * up).astype(BF16)
        acc_ref[...] += _dot(act, wdn_ref[c * FF_CHUNK:(c + 1) * FF_CHUNK, :])
    x = acc_ref[...]
    gate = jax.nn.sigmoid(_dot(_rms(x, gp_ref[...]).astype(BF16), wg_ref[...]))
    x = x + gate * _dot(p_ref[...].astype(BF16), wp_ref[...])
    y_ref[...] = _rms(x, gf_ref[...]) if final else x


def _ffn_ple(x2, t, ffn_norm, w_up, conv_w, conv_b, w_down, p2, ple_norm, w_gate, w_proj,
             final_norm, final, tm):
    n, d = x2.shape
    pd = p2.shape[1]
    nc2 = 2 * N_FF_CHUNKS
    row = lambda i: (i, 0)
    const = lambda i: (0, 0)
    once = pl.Buffered(1)
    return pl.pallas_call(
        functools.partial(_ffn_kernel, tiles_per_seq=t // tm, final=final),
        grid=(n // tm,),
        in_specs=[pl.BlockSpec((tm, d), row), pl.BlockSpec((1, d), const),
                  pl.BlockSpec((d, 2 * D_FF), const, pipeline_mode=once),
                  pl.BlockSpec((CONV_WIDTH, 2 * D_FF), const),
                  pl.BlockSpec((1, 2 * D_FF), const),
                  pl.BlockSpec((D_FF, d), const, pipeline_mode=once),
                  pl.BlockSpec((tm, pd), row), pl.BlockSpec((1, d), const),
                  pl.BlockSpec((d, d), const, pipeline_mode=once),
                  pl.BlockSpec((pd, d), const, pipeline_mode=once),
                  pl.BlockSpec((1, d), const)],
        out_specs=pl.BlockSpec((tm, d), row),
        out_shape=jax.ShapeDtypeStruct((n, d), F32),
        scratch_shapes=[pltpu.VMEM((tm, d), F32), pltpu.VMEM((tm + HALO, FF_CHUNK), F32),
                        pltpu.VMEM((nc2, HALO, FF_CHUNK), F32)],
        compiler_params=_cparams(1), name="conv_ffn_ple",
    )(x2, ffn_norm.reshape(1, d), w_up.astype(BF16), conv_w, conv_b.reshape(1, 2 * D_FF),
      w_down.astype(BF16), p2, ple_norm.reshape(1, d),
      w_gate.astype(BF16), w_proj.astype(BF16), final_norm.reshape(1, d))


PROJ_TM = 512


def kernel(x, p, positions, attn_norm, w_in, w_o, ffn_norm, w_up, conv_w, conv_b, w_down,
           ple_norm, w_ple_gate, w_ple_proj, final_norm):
    b, t, d = x.shape
    depth = p.shape[0]
    n = b * t
    topk = min(TOPK_MAX, t // 4)
    assert t % DILATED_PAIRS[-1][0] == 0 and t % DSA_TQ == 0 and t % PROJ_TM == 0
    x2 = x.reshape(n, d)
    pos2 = positions.reshape(n, 1)
    for i in range(depth):
        za, zb, zi, wi, za4, za16 = _project(x2, pos2, attn_norm[i], w_in[i], b, t, PROJ_TM)
        views = (za.reshape(b, 1, t, 3 * GROUP_W), za4, za16)
        branches = [_dilated_branch(zr, dil) for zr, (_, dil) in zip(views, DILATED_PAIRS)]
        ob = _sparse_attention(zb, zi, wi, b, t, topk)
        x2 = _out_proj(x2, branches, ob, w_o[i], t)
        x2 = _ffn_ple(x2, t, ffn_norm[i], w_up[i], conv_w[i], conv_b[i], w_down[i],
                      p[i].reshape(n, -1), ple_norm[i], w_ple_gate[i], w_ple_proj[i],
                      final_norm, i == depth - 1, PROJ_TM)
    return x2.reshape(b, t, d)
```

```python
import functools
import math

import numpy as np
import jax
import jax.numpy as jnp
from jax import lax
from jax.experimental import pallas as pl
from jax.experimental.pallas import tpu as pltpu

F32 = jnp.float32
BF16 = jnp.bfloat16

HEAD_DIM = 64
HALF = HEAD_DIM // 2
N_HEADS = 8
GROUP_W = N_HEADS * HEAD_DIM
N_PAIRS = N_HEADS // 2
LANES = 128
DILATED_PAIRS = ((128, 1), (512, 4), (2048, 16))
WIN_STEPS = 128
TOPK_MAX = 256
D_FF = 2816
FF_CHUNK = 256
N_FF_CHUNKS = D_FF // FF_CHUNK
CONV_WIDTH = 3
ROPE_THETA = 10000.0
RMS_EPS = 1e-6
MASK_VALUE = -1e30
KEY_NEG_INF = -2139095041
VMEM_LIMIT = 56 * 1024 * 1024


def _cparams(n_axes):
    return pltpu.CompilerParams(dimension_semantics=("arbitrary",) * n_axes,
                                vmem_limit_bytes=VMEM_LIMIT)


def _dot_t(a, b):
    return lax.dot_general(a, b, (((1,), (1,)), ((), ())), preferred_element_type=F32)


def _dot(a, b):
    return jnp.dot(a, b, preferred_element_type=F32)


def _rms(x, g):
    ms = jnp.mean(x * x, axis=-1, keepdims=True)
    return (x * lax.rsqrt(ms + RMS_EPS)) * g


def _pair_perm():
    idx = []
    for j in range(N_PAIRS):
        a, b = 2 * j * HEAD_DIM, (2 * j + 1) * HEAD_DIM
        idx += list(range(a, a + HALF)) + list(range(b, b + HALF))
        idx += list(range(a + HALF, a + HEAD_DIM)) + list(range(b + HALF, b + HEAD_DIM))
    return np.asarray(idx, np.int32)


def _in_proj_columns():
    perm = _pair_perm()
    nat = np.arange(GROUP_W, dtype=np.int32)
    cols = []
    for g, roped in enumerate((True, True, False, True, True, False, True)):
        cols.append(g * GROUP_W + (perm if roped else nat))
    kbase = 7 * GROUP_W
    k1 = kbase + np.arange(HALF, dtype=np.int32)
    k2 = kbase + HALF + np.arange(HALF, dtype=np.int32)
    cols.append(np.concatenate([k1, k1, k2, k2]))
    return np.concatenate(cols)


PERM_ROWS = 256


def _residue_perm(dil):
    p = np.zeros((PERM_ROWS, PERM_ROWS), np.float32)
    per = PERM_ROWS // dil
    for r in range(dil):
        for u in range(per):
            p[r * per + u, u * dil + r] = 1.0
    return p


def _proj_kernel(x_ref, pos_ref, g_ref, invf_ref, w_ref, wiw_ref, p4_ref, p16_ref,
                 za_ref, zb_ref, zi_ref, wi_ref, za4_ref, za16_ref):
    x = x_ref[...]
    h32 = _rms(x, g_ref[...])
    h = h32.astype(BF16)
    ang = pos_ref[...].astype(F32) * invf_ref[...]
    lane = lax.broadcasted_iota(jnp.int32, (1, LANES), 1)
    cos = jnp.cos(ang)
    sin = jnp.sin(ang) * jnp.where(lane < 2 * HALF, -1.0, 1.0)

    def rope(z):
        return z * cos + pltpu.roll(z, 2 * HALF, 1) * sin

    q_scale = HEAD_DIM ** -0.5
    plan = ((za_ref, 0, True, q_scale), (za_ref, 1, True, 1.0), (za_ref, 2, False, 1.0),
            (zb_ref, 0, True, q_scale * math.log2(math.e)), (zb_ref, 1, True, 1.0), (zb_ref, 2, False, 1.0),
            (zi_ref, 0, True, q_scale))
    for g, (out_ref, slot, roped, scale) in enumerate(plan):
        z = _dot(h, w_ref[:, g * GROUP_W:(g + 1) * GROUP_W])
        for j in range(N_PAIRS):
            zj = z[:, j * LANES:(j + 1) * LANES]
            if roped:
                zj = rope(zj)
            if scale != 1.0:
                zj = zj * scale
            out_ref[:, slot * GROUP_W + j * LANES: slot * GROUP_W + (j + 1) * LANES] = zj.astype(BF16)
    zk = _dot(h, w_ref[:, 7 * GROUP_W: 7 * GROUP_W + LANES])
    zi_ref[:, GROUP_W:GROUP_W + LANES] = rope(zk).astype(BF16)
    wi_ref[...] = _dot(h, wiw_ref[...]) * (N_HEADS ** -0.5)
    for half in range(x.shape[0] // PERM_ROWS):
        za = za_ref[half * PERM_ROWS:(half + 1) * PERM_ROWS, :]
        for dil, p_ref, out_ref in ((4, p4_ref, za4_ref), (16, p16_ref, za16_ref)):
            per = PERM_ROWS // dil
            zp = _dot(p_ref[...], za).astype(BF16)
            for r in range(dil):
                out_ref[r, half * per:(half + 1) * per, :] = zp[r * per:(r + 1) * per, :]


def _project(x2, pos2, attn_norm, w_in, b, t, tm):
    n, d = x2.shape
    tiles = t // tm
    res_map = lambda i: (i // tiles, 0, i % tiles, 0)
    cols = _in_proj_columns()
    w_main = jnp.take(w_in, cols, axis=1).astype(BF16)
    w_wi = jnp.pad(w_in[:, 7 * GROUP_W + HEAD_DIM:], ((0, 0), (0, LANES - N_HEADS))).astype(BF16)
    inv_freq = 1.0 / (ROPE_THETA ** (np.arange(0, HEAD_DIM, 2, dtype=np.float32) / HEAD_DIM))
    invf = jnp.asarray(np.tile(inv_freq, 4)[None, :], F32)
    wcols = w_main.shape[1]
    const = lambda i: (0, 0)
    row = lambda i: (i, 0)
    return pl.pallas_call(
        _proj_kernel,
        grid=(n // tm,),
        in_specs=[pl.BlockSpec((tm, d), row), pl.BlockSpec((tm, 1), row),
                  pl.BlockSpec((1, d), const), pl.BlockSpec((1, LANES), const),
                  pl.BlockSpec((d, wcols), const, pipeline_mode=pl.Buffered(1)),
                  pl.BlockSpec((d, LANES), const),
                  pl.BlockSpec((PERM_ROWS, PERM_ROWS), const),
                  pl.BlockSpec((PERM_ROWS, PERM_ROWS), const)],
        out_specs=[pl.BlockSpec((tm, 3 * GROUP_W), row), pl.BlockSpec((tm, 3 * GROUP_W), row),
                   pl.BlockSpec((tm, GROUP_W + LANES), row), pl.BlockSpec((tm, LANES), row),
                   pl.BlockSpec((None, 4, tm // 4, 3 * GROUP_W), res_map),
                   pl.BlockSpec((None, 16, tm // 16, 3 * GROUP_W), res_map)],
        out_shape=[jax.ShapeDtypeStruct((n, 3 * GROUP_W), BF16),
                   jax.ShapeDtypeStruct((n, 3 * GROUP_W), BF16),
                   jax.ShapeDtypeStruct((n, GROUP_W + LANES), BF16),
                   jax.ShapeDtypeStruct((n, LANES), F32),
                   jax.ShapeDtypeStruct((b, 4, t // 4, 3 * GROUP_W), BF16),
                   jax.ShapeDtypeStruct((b, 16, t // 16, 3 * GROUP_W), BF16)],
        compiler_params=_cparams(1), name="in_proj_rope",
    )(x2, pos2, attn_norm.reshape(1, d), invf, w_main, w_wi,
      jnp.asarray(_residue_perm(4), BF16), jnp.asarray(_residue_perm(16), BF16))


def _head_lane_masks():
    lane = lax.broadcasted_iota(jnp.int32, (1, LANES), 1)
    first = (lane % HEAD_DIM) < HALF
    return first, jnp.logical_not(first)


DIL_BLOCKS = 2


def _dilated_kernel(q_ref, kc_ref, kp_ref, vc_ref, vp_ref, o_ref, lse_ref, s_ref):
    g = pl.program_id(2)
    n = WIN_STEPS
    a = lax.broadcasted_iota(jnp.int32, (n, 2 * n), 0)
    c = lax.broadcasted_iota(jnp.int32, (n, 2 * n), 1)
    band = (c >= a) & (c <= a + n)
    band_first = band & ((g > 0) | (c >= n))
    masks = _head_lane_masks()
    lane = lax.broadcasted_iota(jnp.int32, (1, LANES), 1)
    low = lane < HEAD_DIM

    def with_prev(cur_ref, prev_ref, sub, sl):
        before = prev_ref[:, sl] if sub == 0 else cur_ref[(sub - 1) * n:sub * n, sl]
        return jnp.concatenate([before, cur_ref[sub * n:(sub + 1) * n, sl]], axis=0)

    for sub in range(DIL_BLOCKS):
        for h in range(N_HEADS):
            j, hb = divmod(h, 2)
            sl = slice(j * LANES, (j + 1) * LANES)
            q = q_ref[sub * n:(sub + 1) * n, sl]
            qm = jnp.where(masks[hb], q, jnp.zeros_like(q))
            s = _dot_t(qm, with_prev(kc_ref, kp_ref, sub, sl))
            s_ref[sub * N_HEADS + h] = jnp.where(band_first if sub == 0 else band, s, -jnp.inf)
    for sub in range(DIL_BLOCKS):
        rows = slice(sub * n, (sub + 1) * n)
        lse_tile = jnp.zeros((n, LANES), F32)
        for j in range(N_PAIRS):
            sl = slice(j * LANES, (j + 1) * LANES)
            vv = with_prev(vc_ref, vp_ref, sub, sl)
            outs = []
            for hb in range(2):
                s = s_ref[sub * N_HEADS + 2 * j + hb]
                m = jnp.max(s, axis=1, keepdims=True)
                p = jnp.exp(s - m)
                l = jnp.sum(p, axis=1, keepdims=True)
                outs.append(_dot(p.astype(BF16), vv) / l)
                lse_tile = jnp.where(lane == 2 * j + hb, m + jnp.log(l), lse_tile)
            o_ref[rows, sl] = jnp.where(low, outs[0], outs[1]).astype(BF16)
        lse_ref[rows, :] = lse_tile


def _dilated_branch(zr, dil):
    n = WIN_STEPS
    step = DIL_BLOCKS * n
    b, _, rows, _ = zr.shape
    blk = (None, None, step, GROUP_W)
    prev_blk = (None, None, n, GROUP_W)
    cur = lambda slot: (lambda bi, r, g: (bi, r, g, slot))
    prev = lambda slot: (lambda bi, r, g: (bi, r, jnp.maximum(DIL_BLOCKS * g - 1, 0), slot))
    return pl.pallas_call(
        _dilated_kernel,
        grid=(b, dil, rows // step),
        in_specs=[pl.BlockSpec(blk, cur(0)), pl.BlockSpec(blk, cur(1)), pl.BlockSpec(prev_blk, prev(1)),
                  pl.BlockSpec(blk, cur(2)), pl.BlockSpec(prev_blk, prev(2))],
        out_specs=[pl.BlockSpec(blk, cur(0)), pl.BlockSpec((None, None, step, LANES), cur(0))],
        out_shape=[jax.ShapeDtypeStruct((b, dil, rows, GROUP_W), BF16),
                   jax.ShapeDtypeStruct((b, dil, rows, LANES), F32)],
        scratch_shapes=[pltpu.VMEM((DIL_BLOCKS * N_HEADS, n, 2 * n), F32)],
        compiler_params=_cparams(3), name=f"dilated_attn_d{dil}",
    )(zr, zr, zr, zr, zr)


DSA_TQ = 256
P1_CHUNK = 256
P3_CHUNK = 256
CNT_ROWS = 32
BF16_ROWS = 16
VT_ROWS = HEAD_DIM + BF16_ROWS
BF16_MIN_NORMAL = 0x0080
CHECK_EVERY = 2


def _chunk_loop(n, body, init):
    carry = lax.fori_loop(0, n // 2, lambda i, c: body(2 * i + 1, body(2 * i, c)), init)
    return lax.cond(n % 2 == 1, lambda c: body(n - 1, c), lambda c: c, carry)


def _to_key(x):
    bits = pltpu.bitcast(x, jnp.int32)
    return bits ^ ((bits >> 31) & 0x7FFFFFFF)


def _dsa_kernel(qb_ref, qi_ref, wit_ref, kb_ref, vt_ref, ki_ref, ot_ref,
                key_ref, top_ref, qim_ref, qbm_ref, m_ref, acc_ref, thr_ref, s_ref, mx_ref,
                *, topk):
    assert P3_CHUNK == P1_CHUNK == DSA_TQ
    tq = DSA_TQ
    qblk = pl.program_id(1)
    t0 = qblk * tq
    n1 = qblk + 1
    masks = _head_lane_masks()

    for h in range(N_HEADS):
        j, hb = divmod(h, 2)
        sl = slice(j * LANES, (j + 1) * LANES)
        qi = qi_ref[:, sl]
        qb = qb_ref[:, sl]
        qim_ref[h] = jnp.where(masks[hb], qi, jnp.zeros_like(qi))
        qbm_ref[h] = jnp.where(masks[hb], qb, jnp.zeros_like(qb))

    def p1_chunk(c, diagonal):
        k0 = pl.multiple_of(c * P1_CHUNK, P1_CHUNK)
        kic = ki_ref[pl.ds(k0, P1_CHUNK), :]
        acc = jnp.zeros((P1_CHUNK, tq), F32)
        for h in range(N_HEADS):
            acc = acc + wit_ref[h:h + 1, :] * jnp.maximum(_dot_t(kic, qim_ref[h]), 0.0)
        acc = jnp.where(acc == 0.0, 0.0, acc)
        if diagonal:
            kpos = lax.broadcasted_iota(jnp.int32, (P1_CHUNK, tq), 0)
            qpos = lax.broadcasted_iota(jnp.int32, (P1_CHUNK, tq), 1)
            acc = jnp.where(kpos <= qpos, acc, -jnp.inf)
        key_ref[pl.ds(k0, P1_CHUNK), :] = _to_key(acc)
        top = pltpu.bitcast(pltpu.bitcast(acc, jnp.int32) & (-65536), F32)
        top_ref[pl.ds(k0, P1_CHUNK), :] = top.astype(BF16)

    def p1_pair(i, carry):
        p1_chunk(2 * i, False)
        p1_chunk(2 * i + 1, False)
        return carry

    lax.fori_loop(0, qblk // 2, p1_pair, 0)

    @pl.when(qblk % 2 == 1)
    def _():
        p1_chunk(qblk - 1, False)

    p1_chunk(qblk, True)

    def count_ge(thr):
        def body(c, cnt):
            k0 = pl.multiple_of(c * P1_CHUNK, P1_CHUNK)
            x = jnp.where(key_ref[pl.ds(k0, P1_CHUNK), :] >= thr, 1, 0)
            return cnt + jnp.sum(x.reshape(P1_CHUNK // CNT_ROWS, CNT_ROWS, tq), axis=0)

        cnt = _chunk_loop(n1, body, jnp.zeros((CNT_ROWS, tq), jnp.int32))
        return jnp.sum(cnt, axis=0, keepdims=True)

    def count_packed(thr_f):
        thr_b = jnp.broadcast_to(thr_f, (BF16_ROWS, tq)).astype(BF16)[None]
        one, zero = jnp.ones((), BF16), jnp.zeros((), BF16)

        def body(c, cnt):
            k0 = pl.multiple_of(c * P1_CHUNK, P1_CHUNK)
            x = top_ref[pl.ds(k0, P1_CHUNK), :].reshape(P1_CHUNK // BF16_ROWS, BF16_ROWS, tq)
            y = jnp.where(x >= thr_b, one, zero)
            parts = [y[r] for r in range(P1_CHUNK // BF16_ROWS)]
            while len(parts) > 1:
                parts = [a + b for a, b in zip(parts[::2], parts[1::2])]
            return cnt + parts[0].astype(F32)

        cnt = _chunk_loop(n1, body, jnp.zeros((BF16_ROWS, tq), F32))
        return jnp.sum(cnt, axis=0, keepdims=True).astype(jnp.int32)

    def count_ge_top(mid16):
        pattern = mid16 ^ ((mid16 >> 15) & 0x7FFF)
        pattern = jnp.where((mid16 > 0) & (mid16 < BF16_MIN_NORMAL), BF16_MIN_NORMAL, pattern)
        return count_packed(pltpu.bitcast(pattern << 16, F32))

    def step(st, count_fn, final):
        lo, hi, thr, done, cnt_hi = st
        mid = (lo >> 1) + (hi >> 1) + (((lo & 1) + (hi & 1) + 1) >> 1)
        c = count_fn(mid)
        ge = c >= topk
        active = done == 0
        hit = active & (c == topk)
        new_lo = jnp.where(active & ge, mid, lo)
        new_hi = jnp.where(active & jnp.logical_not(ge), mid - 1, hi)
        new_cnt_hi = jnp.where(active & jnp.logical_not(ge), c, cnt_hi)
        new_thr = jnp.where(hit, mid, thr)
        new_done = jnp.where(hit, 1, done)
        if final:
            conv = active & jnp.logical_not(hit) & (new_lo == new_hi)
            new_thr = jnp.where(conv, new_lo, new_thr)
            new_done = jnp.where(conv, 2, new_done)
        return new_lo, new_hi, new_thr, new_done, new_cnt_hi

    t_row = t0 + lax.broadcasted_iota(jnp.int32, (1, tq), 1)
    all_sel = t_row + 1 <= topk
    done0 = all_sel.astype(jnp.int32)
    zeros = jnp.zeros((1, tq), jnp.int32)
    lo16 = jnp.full((1, tq), (KEY_NEG_INF + 1) >> 16, jnp.int32)
    hi16 = jnp.full((1, tq), 0x7F7F, jnp.int32)
    st = lax.fori_loop(0, 16, lambda _, s: step(s, count_ge_top, False),
                       (lo16, hi16, zeros, done0, zeros))
    lo16, _, thr16, done16, cnt_hi = st
    def digits(c, carry):
        k0 = pl.multiple_of(c * P1_CHUNK, P1_CHUNK)
        key = key_ref[pl.ds(k0, P1_CHUNK), :]
        digit = jnp.where((key >> 16) == lo16, (key >> 8) & 0xFF, -1)
        top_ref[pl.ds(k0, P1_CHUNK), :] = digit.astype(F32).astype(BF16)
        return carry

    _chunk_loop(n1, digits, 0)
    above = cnt_hi
    st = lax.fori_loop(0, 8, lambda _, s: step(s, lambda m: above + count_packed(m.astype(F32)), False),
                       (zeros, zeros + 0xFF, zeros, done16, cnt_hi))
    lo8, _, thr8, done, cnt_hi = st
    lo = (lo16 << 16) | (lo8 << 8)
    thr = jnp.where(done16 == 1, thr16 << 16, (lo16 << 16) | (thr8 << 8))
    thr = jnp.where(all_sel, KEY_NEG_INF + 1, thr)

    def fine_cond(carry):
        it, st = carry
        return (it < 8 // CHECK_EVERY) & (jnp.min(st[3]) == 0)

    def fine_body(carry):
        it, st = carry
        return it + 1, lax.fori_loop(0, CHECK_EVERY, lambda _, s: step(s, count_ge, True), st)

    _, st = lax.while_loop(fine_cond, fine_body, (0, (lo, lo + 0xFF, thr, done, cnt_hi)))
    _, _, thr, done, cnt_hi = st
    thr_ref[...] = thr

    tie = done == 2

    @pl.when(jnp.max(done) == 2)
    def _():
        need = topk - cnt_hi
        sub_i = lax.broadcasted_iota(jnp.int32, (P1_CHUNK, tq), 0)

        def count_eq_le(jmax):
            def body(c, cnt):
                k0 = pl.multiple_of(c * P1_CHUNK, P1_CHUNK)
                kc = key_ref[pl.ds(k0, P1_CHUNK), :]
                x = jnp.where((kc == thr) & (sub_i + k0 <= jmax), 1, 0)
                return cnt + jnp.sum(x.reshape(P1_CHUNK // CNT_ROWS, CNT_ROWS, tq), axis=0)

            cnt = lax.fori_loop(0, n1, body, jnp.zeros((CNT_ROWS, tq), jnp.int32))
            return jnp.sum(cnt, axis=0, keepdims=True)

        def jbody(_, st):
            jlo, jhi = st
            jmid = (jlo + jhi) >> 1
            ok = count_eq_le(jmid) >= need
            return jnp.where(ok, jlo, jmid + 1), jnp.where(ok, jmid, jhi)

        jlo, _ = lax.fori_loop(0, 14, jbody, (jnp.zeros((1, tq), jnp.int32),
                                               jnp.broadcast_to(t0 + tq - 1, (1, tq)).astype(jnp.int32)))
        jcut = jnp.where(tie, jlo, 0x7FFFFFFF)

        def demote(c, carry):
            k0 = pl.multiple_of(c * P1_CHUNK, P1_CHUNK)
            kc = key_ref[pl.ds(k0, P1_CHUNK), :]
            key_ref[pl.ds(k0, P1_CHUNK), :] = jnp.where((kc == thr) & (sub_i + k0 > jcut), kc - 1, kc)
            return carry

        lax.fori_loop(0, n1, demote, 0)

    m_ref[...] = jnp.full(m_ref.shape, MASK_VALUE, F32)
    acc_ref[...] = jnp.zeros(acc_ref.shape, F32)

    def scores(c, slot):
        k0 = pl.multiple_of(c * P3_CHUNK, P3_CHUNK)
        sel = key_ref[pl.ds(k0, P3_CHUNK), :] >= thr_ref[...]
        for h in range(N_HEADS):
            j = h // 2
            kc = kb_ref[pl.ds(k0, P3_CHUNK), j * LANES:(j + 1) * LANES]
            s = jnp.where(sel, _dot_t(kc, qbm_ref[h]), MASK_VALUE)
            s_ref[slot, h] = s
            mx_ref[slot, h] = jnp.max(s, axis=0, keepdims=True)

    def accumulate(c, slot):
        k0 = pl.multiple_of(c * P3_CHUNK, P3_CHUNK)
        for h in range(N_HEADS):
            rows = slice(h * VT_ROWS, (h + 1) * VT_ROWS)
            m_old = m_ref[h]
            m_new = jnp.maximum(m_old, mx_ref[slot, h])
            alpha = jnp.exp2(m_old - m_new)
            p = jnp.exp2((s_ref[slot, h] - m_new).astype(BF16))
            m_ref[h] = m_new
            pv = _dot(vt_ref[rows, pl.ds(k0, P3_CHUNK)], p)
            acc_ref[rows, :] = alpha * acc_ref[rows, :] + pv

    scores(0, 0)

    def p3_pair(i, carry):
        c = 2 * i
        scores(c + 1, 1)
        accumulate(c, 0)
        scores(c + 2, 0)
        accumulate(c + 1, 1)
        return carry

    lax.fori_loop(0, (n1 - 1) // 2, p3_pair, 0)

    @pl.when(n1 % 2 == 1)
    def _():
        accumulate(n1 - 1, 0)

    @pl.when(n1 % 2 == 0)
    def _():
        scores(n1 - 1, 1)
        accumulate(n1 - 2, 0)
        accumulate(n1 - 1, 1)

    for h in range(N_HEADS):
        base = h * VT_ROWS
        denom = acc_ref[base + HEAD_DIM:base + HEAD_DIM + 1, :]
        ot_ref[h * HEAD_DIM:(h + 1) * HEAD_DIM, :] = (acc_ref[base:base + HEAD_DIM, :] / denom).astype(BF16)


def _sparse_attention(zb, zi, wi, b, t, topk):
    tq = DSA_TQ
    zb3 = zb.reshape(b, t, 3 * GROUP_W)
    zi3 = zi.reshape(b, t, GROUP_W + LANES)
    vt = zb3[:, :, 2 * GROUP_W:].transpose(0, 2, 1).reshape(b, N_HEADS, HEAD_DIM, t)
    pad = jnp.zeros((b, N_HEADS, VT_ROWS - HEAD_DIM, t), BF16).at[:, :, 0, :].set(1.0)
    vt = jnp.concatenate([vt, pad], axis=2).reshape(b, N_HEADS * VT_ROWS, t)
    wit = wi.reshape(b, t, LANES)[:, :, :N_HEADS].transpose(0, 2, 1)
    qmap = lambda bi, qi: (bi, qi, 0)
    tmap = lambda bi, qi: (bi, 0, qi)
    out_t = pl.pallas_call(
        functools.partial(_dsa_kernel, topk=topk),
        grid=(b, t // tq),
        in_specs=[pl.BlockSpec((None, tq, GROUP_W), qmap),
                  pl.BlockSpec((None, tq, GROUP_W), qmap),
                  pl.BlockSpec((None, N_HEADS, tq), tmap),
                  pl.BlockSpec((None, t, GROUP_W), lambda bi, qi: (bi, 0, 1),
                               pipeline_mode=pl.Buffered(1)),
                  pl.BlockSpec((None, N_HEADS * VT_ROWS, t), lambda bi, qi: (bi, 0, 0),
                               pipeline_mode=pl.Buffered(1)),
                  pl.BlockSpec((None, t, LANES), lambda bi, qi: (bi, 0, GROUP_W // LANES),
                               pipeline_mode=pl.Buffered(1))],
        out_specs=pl.BlockSpec((None, GROUP_W, tq), tmap),
        out_shape=jax.ShapeDtypeStruct((b, GROUP_W, t), BF16),
        scratch_shapes=[pltpu.VMEM((t, tq), jnp.int32),
                        pltpu.VMEM((t, tq), BF16),
                        pltpu.VMEM((N_HEADS, tq, LANES), BF16),
                        pltpu.VMEM((N_HEADS, tq, LANES), BF16),
                        pltpu.VMEM((N_HEADS, 1, tq), F32),
                        pltpu.VMEM((N_HEADS * VT_ROWS, tq), F32),
                        pltpu.VMEM((1, tq), jnp.int32),
                        pltpu.VMEM((2, N_HEADS, P3_CHUNK, tq), F32),
                        pltpu.VMEM((2, N_HEADS, 1, tq), F32)],
        compiler_params=_cparams(2), name="dsa_attn",
    )(zb3, zi3, wit, zb3, vt, zi3)
    return out_t.transpose(0, 2, 1).reshape(b * t, GROUP_W)


def _dot_hi_lo(a, b, f32_side):
    x = a if f32_side == 0 else b
    hi = x.astype(BF16)
    lo = (x - hi.astype(F32)).astype(BF16)
    if f32_side == 0:
        return _dot(hi, b) + _dot(lo, b)
    return _dot(a, hi) + _dot(a, lo)


def _out_proj_kernel(x_ref, o1_ref, l1_ref, o4_ref, l4_ref, o16_ref, l16_ref, ob_ref, wo_ref,
                     pt4_ref, pt16_ref, expand_ref, y_ref):
    o1, l1 = o1_ref[...].astype(F32), l1_ref[...]
    o2 = _dot(pt4_ref[...], o4_ref[...].reshape(PERM_ROWS, GROUP_W))
    o3 = _dot(pt16_ref[...], o16_ref[...].reshape(PERM_ROWS, GROUP_W))
    l2 = _dot_hi_lo(pt4_ref[...], l4_ref[...].reshape(PERM_ROWS, LANES), 1)
    l3 = _dot_hi_lo(pt16_ref[...], l16_ref[...].reshape(PERM_ROWS, LANES), 1)
    m = jnp.maximum(jnp.maximum(l1, l2), l3)
    e1, e2, e3 = jnp.exp(l1 - m), jnp.exp(l2 - m), jnp.exp(l3 - m)
    inv = 1.0 / (e1 + e2 + e3)
    w1, w2, w3 = (_dot_hi_lo(e * inv, expand_ref[...], 0) for e in (e1, e2, e3))
    out_a = w1 * o1 + w2 * o2 + w3 * o3
    y = x_ref[...] + _dot(out_a.astype(BF16), wo_ref[0:GROUP_W, :])
    y_ref[...] = y + _dot(ob_ref[...], wo_ref[GROUP_W:2 * GROUP_W, :])


def _out_proj(x2, branches, ob, w_o, t):
    n, d = x2.shape
    tm = PERM_ROWS
    tiles = t // tm
    row = lambda i: (i, 0)
    const = lambda i: (0, 0)
    res_map = lambda i: (i // tiles, 0, i % tiles, 0)
    ospec = lambda dil: pl.BlockSpec((None, dil, tm // dil, GROUP_W), res_map)
    lspec = lambda dil: pl.BlockSpec((None, dil, tm // dil, LANES), res_map)
    pspec = pl.BlockSpec((tm, tm), const)
    (o1, l1), (o4, l4), (o16, l16) = branches
    expand = np.zeros((LANES, GROUP_W), np.float32)
    for h in range(N_HEADS):
        expand[h, h * HEAD_DIM:(h + 1) * HEAD_DIM] = 1.0
    return pl.pallas_call(
        _out_proj_kernel,
        grid=(n // tm,),
        in_specs=[pl.BlockSpec((tm, d), row), pl.BlockSpec((tm, GROUP_W), row),
                  pl.BlockSpec((tm, LANES), row), ospec(4), lspec(4), ospec(16), lspec(16),
                  pl.BlockSpec((tm, GROUP_W), row), pl.BlockSpec((2 * GROUP_W, d), const), pspec, pspec,
                  pl.BlockSpec((LANES, GROUP_W), const)],
        out_specs=pl.BlockSpec((tm, d), row),
        out_shape=jax.ShapeDtypeStruct((n, d), F32),
        compiler_params=_cparams(1), name="merge_out_proj",
    )(x2, o1.reshape(n, GROUP_W), l1.reshape(n, LANES), o4, l4, o16, l16, ob, w_o.astype(BF16),
      jnp.asarray(_residue_perm(4).T, BF16), jnp.asarray(_residue_perm(16).T, BF16),
      jnp.asarray(expand, BF16))


HALO = 8


def _ffn_kernel(x_ref, g_ref, wup_ref, cw_ref, cb_ref, wdn_ref, p_ref, gp_ref, wg_ref, wp_ref, gf_ref,
                y_ref, acc_ref, buf_ref, tail_ref, *, tiles_per_seq, final):
    tm = x_ref.shape[0]
    i = pl.program_id(0)

    @pl.when(i % tiles_per_seq == 0)
    def _():
        tail_ref[...] = jnp.zeros(tail_ref.shape, F32)

    x = x_ref[...]
    h = _rms(x, g_ref[...]).astype(BF16)
    acc_ref[...] = x

    def conv(c):
        cols = slice(c * FF_CHUNK, (c + 1) * FF_CHUNK)
        u = _dot(h, wup_ref[:, cols])
        buf_ref[0:HALO, :] = tail_ref[c]
        buf_ref[HALO:HALO + tm, :] = u
        tail_ref[c] = u[tm - HALO:, :]
        y = cb_ref[:, cols] + cw_ref[2:3, cols] * u
        y = y + cw_ref[1:2, cols] * buf_ref[HALO - 1:HALO - 1 + tm, :]
        return y + cw_ref[0:1, cols] * buf_ref[HALO - 2:HALO - 2 + tm, :]

    for c in range(N_FF_CHUNKS):
        gate = conv(c)
        up = conv(c + N_FF_CHUNKS)
        act = (gate * jax.nn.sigmoid(gate) * up).astype(BF16)
        acc_ref[...] += _dot(act, wdn_ref[c * FF_CHUNK:(c + 1) * FF_CHUNK, :])
    x = acc_ref[...]
    gate = jax.nn.sigmoid(_dot(_rms(x, gp_ref[...]).astype(BF16), wg_ref[...]))
    x = x + gate * _dot(p_ref[...].astype(BF16), wp_ref[...])
    y_ref[...] = _rms(x, gf_ref[...]) if final else x


def _ffn_ple(x2, t, ffn_norm, w_up, conv_w, conv_b, w_down, p2, ple_norm, w_gate, w_proj,
             final_norm, final, tm):
    n, d = x2.shape
    pd = p2.shape[1]
    nc2 = 2 * N_FF_CHUNKS
    row = lambda i: (i, 0)
    const = lambda i: (0, 0)
    once = pl.Buffered(1)
    return pl.pallas_call(
        functools.partial(_ffn_kernel, tiles_per_seq=t // tm, final=final),
        grid=(n // tm,),
        in_specs=[pl.BlockSpec((tm, d), row), pl.BlockSpec((1, d), const),
                  pl.BlockSpec((d, 2 * D_FF), const, pipeline_mode=once),
                  pl.BlockSpec((CONV_WIDTH, 2 * D_FF), const),
                  pl.BlockSpec((1, 2 * D_FF), const),
                  pl.BlockSpec((D_FF, d), const, pipeline_mode=once),
                  pl.BlockSpec((tm, pd), row), pl.BlockSpec((1, d), const),
                  pl.BlockSpec((d, d), const, pipeline_mode=once),
                  pl.BlockSpec((pd, d), const, pipeline_mode=once),
                  pl.BlockSpec((1, d), const)],
        out_specs=pl.BlockSpec((tm, d), row),
        out_shape=jax.ShapeDtypeStruct((n, d), F32),
        scratch_shapes=[pltpu.VMEM((tm, d), F32), pltpu.VMEM((tm + HALO, FF_CHUNK), F32),
                        pltpu.VMEM((nc2, HALO, FF_CHUNK), F32)],
        compiler_params=_cparams(1), name="conv_ffn_ple",
    )(x2, ffn_norm.reshape(1, d), w_up.astype(BF16), conv_w, conv_b.reshape(1, 2 * D_FF),
      w_down.astype(BF16), p2, ple_norm.reshape(1, d),
      w_gate.astype(BF16), w_proj.astype(BF16), final_norm.reshape(1, d))


PROJ_TM = 512


def kernel(x, p, positions, attn_norm, w_in, w_o, ffn_norm, w_up, conv_w, conv_b, w_down,
           ple_norm, w_ple_gate, w_ple_proj, final_norm):
    b, t, d = x.shape
    depth = p.shape[0]
    n = b * t
    topk = min(TOPK_MAX, t // 4)
    assert t % (DIL_BLOCKS * DILATED_PAIRS[-1][0]) == 0 and t % DSA_TQ == 0 and t % PROJ_TM == 0
    x2 = x.reshape(n, d)
    pos2 = positions.reshape(n, 1)
    for i in range(depth):
        za, zb, zi, wi, za4, za16 = _project(x2, pos2, attn_norm[i], w_in[i], b, t, PROJ_TM)
        views = (za.reshape(b, 1, t, 3 * GROUP_W), za4, za16)
        branches = [_dilated_branch(zr, dil) for zr, (_, dil) in zip(views, DILATED_PAIRS)]
        ob = _sparse_attention(zb, zi, wi, b, t, topk)
        x2 = _out_proj(x2, branches, ob, w_o[i], t)
        x2 = _ffn_ple(x2, t, ffn_norm[i], w_up[i], conv_w[i], conv_b[i], w_down[i],
                      p[i].reshape(n, -1), ple_norm[i], w_ple_gate[i], w_ple_proj[i],
                      final_norm, i == depth - 1, PROJ_TM)
    return x2.reshape(b, t, d)
```
